```python
import jax, jax.numpy as jnp
from jax import lax
import numpy as np

D_MODEL = 1024
BATCH = 4
SEQ = 8192
DEPTH = 4

CHUNK = 64
N_MEM = 256
D_MIX = D_MODEL
D_A = D_MIX // 2
D_B = D_MIX - D_A
HGRN_KDIM = 128
HGRN_HEADS = D_A // HGRN_KDIM
HGRN_VDIM = D_A // HGRN_HEADS
POOL_WINDOWS = (2, 4, 8, 16)
POOL_GROUPS = len(POOL_WINDOWS)
POOL_CH = D_B // POOL_GROUPS
D_IN = 4 * D_A + D_B
D_FF = ((8 * D_MODEL // 3) + 127) // 128 * 128
XA_HEADS = 4
XA_HEAD_DIM = D_MODEL // XA_HEADS
EPS = 1e-6

kernel_name = 'hymba_hgrn2_pool_macaron_memxattn'


def rmsnorm(x, g):
    xf = x.astype(jnp.float32)
    y = xf * lax.rsqrt(jnp.mean(xf * xf, axis=-1, keepdims=True) + EPS)
    return (y * g.astype(jnp.float32)).astype(x.dtype)


def swiglu(h, w1, w3, w2):
    return (jax.nn.silu(h @ w1) * (h @ w3)) @ w2


def hgrn2_chunkwise(q, k, v, log_f):
    B, T, H, K = q.shape
    V = v.shape[-1]
    nc = T // CHUNK

    def to_chunks(a):
        return a.reshape(B, nc, CHUNK, H, a.shape[-1]).transpose(1, 0, 3, 2, 4)

    causal = jnp.tril(jnp.ones((CHUNK, CHUNK), dtype=bool))[:, :, None]

    def step(S, inp):
        qc, kc, vc, gc = inp
        b = jnp.cumsum(gc, axis=2)
        o_inter = jnp.einsum('bhtk,bhkv->bhtv', qc * jnp.exp(b), S)
        rel = b[:, :, :, None, :] - b[:, :, None, :, :]
        decay = jnp.exp(jnp.where(causal, rel, -jnp.inf))
        A = jnp.einsum('bhtk,bhsk,bhtsk->bhts', qc, kc, decay)
        o = o_inter + jnp.einsum('bhts,bhsv->bhtv', A, vc)
        b_last = b[:, :, -1:, :]
        S = jnp.exp(b_last[:, :, 0, :, None]) * S + jnp.einsum(
            'bhsk,bhsv->bhkv', kc * jnp.exp(b_last - b), vc)
        return S, o

    S0 = jnp.zeros((B, H, K, V), jnp.float32)
    _, o = lax.scan(step, S0, (to_chunks(q), to_chunks(k), to_chunks(v), to_chunks(log_f)))
    return o.transpose(1, 0, 3, 2, 4).reshape(B, T, H, V)


def hgrn2_mixer(q_in, f_in, i_in, g_in, lb, norm_g):
    B, T, _ = q_in.shape
    shp = (B, T, HGRN_HEADS, HGRN_KDIM)
    q = q_in.astype(jnp.float32).reshape(shp)
    z = f_in.astype(jnp.float32).reshape(shp)
    lbh = lb.reshape(HGRN_HEADS, HGRN_KDIM)
    log_f = jnp.logaddexp(jnp.log(lbh), jnp.log1p(-lbh) + jax.nn.log_sigmoid(z))
    k = (1.0 - lbh) * jax.nn.sigmoid(-z)
    v = i_in.astype(jnp.float32).reshape(B, T, HGRN_HEADS, HGRN_VDIM)
    o = hgrn2_chunkwise(q, k, v, log_f)
    o = o * lax.rsqrt(jnp.mean(o * o, axis=-1, keepdims=True) + EPS)
    o = o * norm_g.astype(jnp.float32).reshape(HGRN_HEADS, HGRN_VDIM)
    o = o.reshape(B, T, D_A) * jax.nn.silu(g_in.astype(jnp.float32))
    return o.astype(q_in.dtype)


def multiscale_pool(u, pool_w, pool_scale):
    B, T, _ = u.shape
    uf = u.astype(jnp.float32).reshape(B, T, POOL_GROUPS, POOL_CH)
    cs = jnp.cumsum(uf, axis=1)
    t_count = jnp.arange(1, T + 1, dtype=jnp.float32)
    outs = []
    for j, w in enumerate(POOL_WINDOWS):
        csj = cs[:, :, j]
        prev = jnp.pad(csj, ((0, 0), (w, 0), (0, 0)))[:, :T]
        mean = (csj - prev) / jnp.minimum(t_count, w)[None, :, None]
        outs.append(mean - uf[:, :, j])
    p = jnp.stack(outs, axis=2)
    y = jnp.einsum('btgc,gcd->btgd', p, pool_w.astype(jnp.float32)).reshape(B, T, D_B)
    return (y * pool_scale.astype(jnp.float32)).astype(u.dtype)


def mem_cross_attention(h, mem_n, wq, wkv, wo):
    B, T, _ = h.shape
    M = mem_n.shape[1]
    q = (h @ wq).reshape(B, T, XA_HEADS, XA_HEAD_DIM)
    kv = (mem_n @ wkv).reshape(B, M, 2, XA_HEADS, XA_HEAD_DIM)
    k, v = kv[:, :, 0], kv[:, :, 1]
    s = jnp.einsum('bthd,bmhd->bhtm', q, k).astype(jnp.float32) * (XA_HEAD_DIM ** -0.5)
    p = jax.nn.softmax(s, axis=-1).astype(h.dtype)
    o = jnp.einsum('bhtm,bmhd->bthd', p, v).reshape(B, T, D_MODEL)
    return o @ wo


def setup_inputs(seed: int = 0) -> dict:
    key = jax.random.key(seed)
    ks = jax.random.split(key, 24)
    f32 = jnp.float32

    def nrm(k, shape, scale):
        return jax.random.normal(k, shape, f32) * scale

    def gain(k, shape):
        return 1.0 + 0.1 * jax.random.normal(k, shape, f32)

    L, D, F = DEPTH, D_MODEL, D_FF
    return {
        'x': nrm(ks[0], (BATCH, SEQ, D), 1.0),
        'mem': nrm(ks[1], (BATCH, N_MEM, D), 1.0),
        'ffn1_norm': gain(ks[2], (L, D)),
        'ffn1_w1': nrm(ks[3], (L, D, F), D ** -0.5),
        'ffn1_w3': nrm(ks[4], (L, D, F), D ** -0.5),
        'ffn1_w2': nrm(ks[5], (L, F, D), F ** -0.5),
        'mix_norm': gain(ks[6], (L, D)),
        'w_in': nrm(ks[7], (L, D, D_IN), D ** -0.5),
        'lb_logits': nrm(ks[8], (L, D_A), 1.0),
        'hgrn_norm': gain(ks[9], (L, D_A)),
        'pool_w': nrm(ks[10], (L, POOL_GROUPS, POOL_CH, POOL_CH), POOL_CH ** -0.5),
        'pool_scale': gain(ks[11], (L, D_B)),
        'w_out': nrm(ks[12], (L, D_MIX, D), D_MIX ** -0.5),
        'xa_norm': gain(ks[13], (L, D)),
        'mem_norm': gain(ks[14], (L, D)),
        'xa_wq': nrm(ks[15], (L, D, D), D ** -0.5),
        'xa_wkv': nrm(ks[16], (L, D, 2 * D), D ** -0.5),
        'xa_wo': nrm(ks[17], (L, D, D), D ** -0.5),
        'ffn2_norm': gain(ks[18], (L, D)),
        'ffn2_w1': nrm(ks[19], (L, D, F), D ** -0.5),
        'ffn2_w3': nrm(ks[20], (L, D, F), D ** -0.5),
        'ffn2_w2': nrm(ks[21], (L, F, D), F ** -0.5),
        'final_norm': gain(ks[22], (D,)),
    }


def reference(x, mem, ffn1_norm, ffn1_w1, ffn1_w3, ffn1_w2, mix_norm, w_in, lb_logits,
              hgrn_norm, pool_w, pool_scale, w_out, xa_norm, mem_norm, xa_wq, xa_wkv,
              xa_wo, ffn2_norm, ffn2_w1, ffn2_w3, ffn2_w2, final_norm):
    lb_all = jnp.cumsum(jax.nn.softmax(lb_logits.astype(jnp.float32), axis=0), axis=0)
    lb_all = lb_all - lb_all[0:1]
    splits = [D_A, 2 * D_A, 3 * D_A, 4 * D_A]
    for l in range(DEPTH):
        x = x + 0.5 * swiglu(rmsnorm(x, ffn1_norm[l]), ffn1_w1[l], ffn1_w3[l], ffn1_w2[l])
        h = rmsnorm(x, mix_norm[l])
        proj = h @ w_in[l]
        q_a, f_a, i_a, g_a, u_b = jnp.split(proj, splits, axis=-1)
        o_a = hgrn2_mixer(q_a, f_a, i_a, g_a, lb_all[l], hgrn_norm[l])
        o_b = multiscale_pool(u_b, pool_w[l], pool_scale[l])
        x = x + jnp.concatenate([o_a, o_b], axis=-1) @ w_out[l]
        x = x + mem_cross_attention(rmsnorm(x, xa_norm[l]), rmsnorm(mem, mem_norm[l]),
                                    xa_wq[l], xa_wkv[l], xa_wo[l])
        x = x + 0.5 * swiglu(rmsnorm(x, ffn2_norm[l]), ffn2_w1[l], ffn2_w3[l], ffn2_w2[l])
    return rmsnorm(x, final_norm)
```

```python
import functools

import jax
import jax.numpy as jnp
import numpy as np
from jax import lax
from jax.experimental import pallas as pl
from jax.experimental.pallas import tpu as pltpu

EPS = 1e-6
HGRN_HEADS = 4
HGRN_KDIM = 128
POOL_WINDOWS = (2, 4, 8, 16)
POOL_CH = 128
POOL_HALO = 16
XA_HEADS = 4
SUBLANES = 8

TIME_TILE = 512
HGRN_CHUNK = 256
VMEM_LIMIT_BYTES = 56 * 1024 * 1024

_BF16 = jnp.bfloat16
_F32 = jnp.float32


def _dot(a, b):
    return jnp.dot(a, b, preferred_element_type=_F32)


def _dot_nt(a, b):
    return lax.dot_general(a, b, (((1,), (1,)), ((), ())), preferred_element_type=_F32)


def _dot_tn(a, b):
    return lax.dot_general(a, b, (((0,), (0,)), ((), ())), preferred_element_type=_F32)


def _rmsnorm(x, g):
    ms = jnp.mean(x * x, axis=-1, keepdims=True)
    return x * lax.rsqrt(ms + EPS) * g


def _swiglu_half_step(x, g, w1_ref, w3_ref, w2_ref):
    hn = _rmsnorm(x, g).astype(_BF16)
    h1 = _dot(hn, w1_ref[...])
    h3 = _dot(hn, w3_ref[...])
    a = (h1 * jax.nn.sigmoid(h1) * h3).astype(_BF16)
    return x + 0.5 * _dot(a, w2_ref[...])


def _level_tables(chunk):
    n_levels = int(np.log2(chunk))
    t = np.arange(chunk)[:, None]
    s = np.arange(chunk)[None, :]
    x = np.bitwise_xor(t, s)
    hb = np.floor(np.log2(np.maximum(x, 1))).astype(np.int32)
    lvl = np.where(t > s, hb, np.where(t == s, n_levels, -1)).astype(np.int32)
    tril = (t >= s).astype(np.float32)
    return lvl, tril, n_levels


def _boundary_rows(b, level_bit):
    c, w = b.shape
    m = 1 << level_bit
    if 2 * m >= 2 * SUBLANES:
        nb = c // (2 * m)
        b3 = b.reshape(nb, 2 * m, w)
        return jnp.broadcast_to(b3[:, m - 1:m, :], (nb, 2 * m, w)).reshape(c, w)
    b3 = b.reshape(c // SUBLANES, SUBLANES, w)
    sub = lax.broadcasted_iota(jnp.int32, b3.shape, 1)

    def row(i):
        return jnp.broadcast_to(b3[:, i:i + 1, :], b3.shape)

    if m == 4:
        out = row(3)
    elif m == 2:
        out = jnp.where(sub < 4, row(1), row(5))
    else:
        out = jnp.where(sub < 2, row(0), jnp.where(sub < 4, row(2), jnp.where(sub < 6, row(4), row(6))))
    return out.reshape(c, w)


def _hgrn_chunk(q, z, v, lb_rows, lvl, tril_bf, st_ref, n_levels):
    c, w = q.shape
    log_lb, log1m_lb, one_m_lb = lb_rows
    log_sig = jnp.minimum(z, 0.0) - jnp.log1p(jnp.exp(-jnp.abs(z)))
    cterm = log1m_lb + log_sig
    mx = jnp.maximum(log_lb, cterm)
    g = mx + jnp.log1p(jnp.exp(-jnp.abs(log_lb - cterm)))
    kk = one_m_lb * jax.nn.sigmoid(-z)

    g_hi = g.astype(_BF16)
    r1 = g - g_hi.astype(_F32)
    g_mid = r1.astype(_BF16)
    g_lo = (r1 - g_mid.astype(_F32)).astype(_BF16)
    b = _dot(tril_bf, g_hi) + _dot(tril_bf, g_mid) + _dot(tril_bf, g_lo)

    t_idx = lax.broadcasted_iota(jnp.int32, (c, w), 0)
    qs, ks = [], []
    for bit in range(n_levels):
        bb = _boundary_rows(b, bit)
        is_right = (t_idx & (1 << bit)) != 0
        e = jnp.exp(jnp.where(is_right, b - bb, bb - b))
        qs.append((q * e).astype(_BF16))
        ks.append((kk * e).astype(_BF16))
    q_bf = q.astype(_BF16)
    k_bf = kk.astype(_BF16)
    v_bf = v.astype(_BF16)

    b_last = b[c - 1:c, :]
    q_in = (q * jnp.exp(b)).astype(_BF16)
    k_out = (kk * jnp.exp(b_last - b)).astype(_BF16)
    s_decay = jnp.exp(b_last)

    outs = []
    for h in range(HGRN_HEADS):
        sl = slice(h * HGRN_KDIM, (h + 1) * HGRN_KDIM)
        a = jnp.where(lvl == n_levels, _dot_nt(q_bf[:, sl], k_bf[:, sl]), 0.0)
        for bit in range(n_levels):
            a = jnp.where(lvl == bit, _dot_nt(qs[bit][:, sl], ks[bit][:, sl]), a)
        st = st_ref[h]
        o = _dot(a.astype(_BF16), v_bf[:, sl]) + _dot_nt(q_in[:, sl], st.astype(_BF16))
        st_ref[h] = st * s_decay[:, sl] + _dot_tn(v_bf[:, sl], k_out[:, sl])
        outs.append(o)
    return outs


def _mixer_kernel(x_ref, g1_ref, w1_ref, w3_ref, w2_ref, gm_ref, win_ref, lb_ref, hn_ref, pw_ref, ps_ref,
                  wout_ref, lvl_ref, tril_ref, o_ref, st_ref, uext_ref, mix_ref, *, chunk, n_levels):
    tt = x_ref.shape[0]
    d_a = HGRN_HEADS * HGRN_KDIM
    t_step = pl.program_id(1)

    @pl.when(t_step == 0)
    def _():
        st_ref[...] = jnp.zeros_like(st_ref)
        uext_ref[0:POOL_HALO, :] = jnp.zeros((POOL_HALO, uext_ref.shape[1]), _F32)

    x1 = _swiglu_half_step(x_ref[...], g1_ref[...], w1_ref, w3_ref, w2_ref)
    h = _rmsnorm(x1, gm_ref[...]).astype(_BF16)
    proj = _dot(h, win_ref[...])

    lb = lb_ref[...]
    lb_rows = (jnp.log(lb), jnp.log1p(-lb), 1.0 - lb)
    lvl = lvl_ref[...]
    tril_bf = tril_ref[...]
    hgrn_g = hn_ref[...]
    for ci in range(tt // chunk):
        rows = slice(ci * chunk, (ci + 1) * chunk)
        outs = _hgrn_chunk(proj[rows, 0:d_a], proj[rows, d_a:2 * d_a], proj[rows, 2 * d_a:3 * d_a],
                           lb_rows, lvl, tril_bf, st_ref, n_levels)
        for hh, o in enumerate(outs):
            sl = slice(hh * HGRN_KDIM, (hh + 1) * HGRN_KDIM)
            o = o * lax.rsqrt(jnp.mean(o * o, axis=-1, keepdims=True) + EPS) * hgrn_g[:, sl]
            gate = proj[rows, 3 * d_a + hh * HGRN_KDIM:3 * d_a + (hh + 1) * HGRN_KDIM]
            mix_ref[rows, sl] = (o * (gate * jax.nn.sigmoid(gate))).astype(_BF16)

    uext_ref[POOL_HALO:POOL_HALO + tt, :] = proj[:, 4 * d_a:]
    t_glob = t_step * tt + lax.broadcasted_iota(jnp.int32, (tt, 1), 0) + 1
    for j, win in enumerate(POOL_WINDOWS):
        cols = slice(j * POOL_CH, (j + 1) * POOL_CH)
        acc = uext_ref[POOL_HALO:POOL_HALO + tt, cols]
        u_j = acc
        for dlt in range(1, win):
            acc = acc + uext_ref[POOL_HALO - dlt:POOL_HALO - dlt + tt, cols]
        cnt = jnp.minimum(t_glob, win).astype(_F32)
        p = acc / cnt - u_j
        y = _dot(p.astype(_BF16), pw_ref[j]) * ps_ref[:, cols]
        mix_ref[:, d_a + j * POOL_CH:d_a + (j + 1) * POOL_CH] = y.astype(_BF16)
    uext_ref[0:POOL_HALO, :] = uext_ref[tt:tt + POOL_HALO, :]

    o_ref[...] = x1 + _dot(mix_ref[...], wout_ref[...])


def _attn_kernel(x_ref, kv_ref, gx_ref, wq_ref, wo_ref, g2_ref, w1_ref, w3_ref, w2_ref, gf_ref, o_ref, att_ref,
                 *, final_norm):
    x2 = x_ref[...]
    d = x2.shape[1]
    hd = d // XA_HEADS
    h = _rmsnorm(x2, gx_ref[...]).astype(_BF16)
    q = _dot(h, wq_ref[...]).astype(_BF16)
    for hh in range(XA_HEADS):
        k = kv_ref[:, hh * hd:(hh + 1) * hd]
        v = kv_ref[:, d + hh * hd:d + (hh + 1) * hd]
        s = _dot_nt(q[:, hh * hd:(hh + 1) * hd], k) * (hd ** -0.5)
        s = s - jnp.max(s, axis=-1, keepdims=True)
        p = jnp.exp(s)
        p = p / jnp.sum(p, axis=-1, keepdims=True)
        att_ref[:, hh * hd:(hh + 1) * hd] = _dot(p.astype(_BF16), v).astype(_BF16)
    x3 = x2 + _dot(att_ref[...], wo_ref[...])
    x4 = _swiglu_half_step(x3, g2_ref[...], w1_ref, w3_ref, w2_ref)
    if final_norm:
        x4 = _rmsnorm(x4, gf_ref[...])
    o_ref[...] = x4


def _kv_kernel(mem_ref, g_ref, wkv_ref, o_ref):
    mn = _rmsnorm(mem_ref[...], g_ref[...]).astype(_BF16)
    o_ref[...] = _dot(mn, wkv_ref[...]).astype(_BF16)


def _resident(shape, index_map):
    return pl.BlockSpec(shape, index_map, pipeline_mode=pl.Buffered(1))


def _mixer_call(x, l, p, lvl, tril, n_levels, tt, chunk):
    bsz, t, d = x.shape
    f = p['ffn1_w1'].shape[2]
    d_in = p['w_in'].shape[2]
    d_a = HGRN_HEADS * HGRN_KDIM
    d_b = d_in - 4 * d_a
    lsel = lambda *shape_tail: (lambda b, i: (l,) + (0,) * len(shape_tail))
    in_specs = [
        pl.BlockSpec((None, tt, d), lambda b, i: (b, i, 0)),
        _resident((None, 1, d), lsel(1, d)),
        _resident((None, d, f), lsel(d, f)),
        _resident((None, d, f), lsel(d, f)),
        _resident((None, f, d), lsel(f, d)),
        _resident((None, 1, d), lsel(1, d)),
        _resident((None, d, d_in), lsel(d, d_in)),
        _resident((None, 1, d_a), lsel(1, d_a)),
        _resident((None, 1, d_a), lsel(1, d_a)),
        _resident((None, len(POOL_WINDOWS), POOL_CH, POOL_CH), lsel(4, POOL_CH, POOL_CH)),
        _resident((None, 1, d_b), lsel(1, d_b)),
        _resident((None, d, d), lsel(d, d)),
        _resident((chunk, chunk), lambda b, i: (0, 0)),
        _resident((chunk, chunk), lambda b, i: (0, 0)),
    ]
    return pl.pallas_call(
        functools.partial(_mixer_kernel, chunk=chunk, n_levels=n_levels),
        grid=(bsz, t // tt),
        in_specs=in_specs,
        out_specs=pl.BlockSpec((None, tt, d), lambda b, i: (b, i, 0)),
        out_shape=jax.ShapeDtypeStruct(x.shape, _F32),
        scratch_shapes=[
            pltpu.VMEM((HGRN_HEADS, HGRN_KDIM, HGRN_KDIM), _F32),
            pltpu.VMEM((POOL_HALO + tt, d_b), _F32),
            pltpu.VMEM((tt, d), _BF16),
        ],
        compiler_params=pltpu.CompilerParams(
            dimension_semantics=("arbitrary", "arbitrary"), vmem_limit_bytes=VMEM_LIMIT_BYTES),
        name=f"mixer_l{l}",
    )(x, p['ffn1_norm'], p['ffn1_w1'], p['ffn1_w3'], p['ffn1_w2'], p['mix_norm'], p['w_in'], p['lb'],
      p['hgrn_norm'], p['pool_w'], p['pool_scale'], p['w_out'], lvl, tril)


def _attn_call(x, kv, l, p, tt, final_norm):
    bsz, t, d = x.shape
    f = p['ffn2_w1'].shape[2]
    n_mem = kv.shape[2]
    lsel = lambda *shape_tail: (lambda b, i: (l,) + (0,) * len(shape_tail))
    in_specs = [
        pl.BlockSpec((None, tt, d), lambda b, i: (b, i, 0)),
        pl.BlockSpec((None, None, n_mem, 2 * d), lambda b, i: (l, b, 0, 0)),
        _resident((None, 1, d), lsel(1, d)),
        _resident((None, d, d), lsel(d, d)),
        _resident((None, d, d), lsel(d, d)),
        _resident((None, 1, d), lsel(1, d)),
        _resident((None, d, f), lsel(d, f)),
        _resident((None, d, f), lsel(d, f)),
        _resident((None, f, d), lsel(f, d)),
        _resident((1, d), lambda b, i: (0, 0)),
    ]
    return pl.pallas_call(
        functools.partial(_attn_kernel, final_norm=final_norm),
        grid=(bsz, t // tt),
        in_specs=in_specs,
        out_specs=pl.BlockSpec((None, tt, d), lambda b, i: (b, i, 0)),
        out_shape=jax.ShapeDtypeStruct(x.shape, _F32),
        scratch_shapes=[pltpu.VMEM((tt, d), _BF16)],
        compiler_params=pltpu.CompilerParams(
            dimension_semantics=("arbitrary", "arbitrary"), vmem_limit_bytes=VMEM_LIMIT_BYTES),
        name=f"attn_l{l}",
    )(x, kv, p['xa_norm'], p['xa_wq'], p['xa_wo'], p['ffn2_norm'], p['ffn2_w1'], p['ffn2_w3'], p['ffn2_w2'],
      p['final_norm'])


def _kv_call(mem, mem_norm, wkv):
    bsz, n_mem, d = mem.shape
    depth = wkv.shape[0]
    return pl.pallas_call(
        _kv_kernel,
        grid=(depth, bsz),
        in_specs=[
            pl.BlockSpec((None, n_mem, d), lambda l, b: (b, 0, 0)),
            pl.BlockSpec((None, 1, d), lambda l, b: (l, 0, 0)),
            pl.BlockSpec((None, d, 2 * d), lambda l, b: (l, 0, 0)),
        ],
        out_specs=pl.BlockSpec((None, None, n_mem, 2 * d), lambda l, b: (l, b, 0, 0)),
        out_shape=jax.ShapeDtypeStruct((depth, bsz, n_mem, 2 * d), _BF16),
        compiler_params=pltpu.CompilerParams(
            dimension_semantics=("arbitrary", "arbitrary"), vmem_limit_bytes=VMEM_LIMIT_BYTES),
        name="mem_kv",
    )(mem, mem_norm, wkv)


def kernel(x, mem, ffn1_norm, ffn1_w1, ffn1_w3, ffn1_w2, mix_norm, w_in, lb_logits, hgrn_norm, pool_w, pool_scale,
           w_out, xa_norm, mem_norm, xa_wq, xa_wkv, xa_wo, ffn2_norm, ffn2_w1, ffn2_w3, ffn2_w2, final_norm):
    depth = w_in.shape[0]
    t = x.shape[1]
    tt = min(TIME_TILE, t)
    chunk = min(HGRN_CHUNK, tt)
    lvl_np, tril_np, n_levels = _level_tables(chunk)
    lvl = jnp.asarray(lvl_np)
    tril = jnp.asarray(tril_np, dtype=_BF16)

    lb_all = jnp.cumsum(jax.nn.softmax(lb_logits.astype(_F32), axis=0), axis=0)
    lb_all = lb_all - lb_all[0:1]

    row = lambda a: a.astype(_F32)[:, None, :]
    bf = lambda a: a.astype(_BF16)
    p = {
        'ffn1_norm': row(ffn1_norm), 'ffn1_w1': bf(ffn1_w1), 'ffn1_w3': bf(ffn1_w3), 'ffn1_w2': bf(ffn1_w2),
        'mix_norm': row(mix_norm), 'w_in': bf(w_in), 'lb': row(lb_all), 'hgrn_norm': row(hgrn_norm),
        'pool_w': bf(pool_w), 'pool_scale': row(pool_scale), 'w_out': bf(w_out),
        'xa_norm': row(xa_norm), 'xa_wq': bf(xa_wq), 'xa_wo': bf(xa_wo),
        'ffn2_norm': row(ffn2_norm), 'ffn2_w1': bf(ffn2_w1), 'ffn2_w3': bf(ffn2_w3), 'ffn2_w2': bf(ffn2_w2),
        'final_norm': final_norm.astype(_F32)[None, :],
    }
    kv = _kv_call(mem, row(mem_norm), bf(xa_wkv))
    for l in range(depth):
        x = _mixer_call(x, l, p, lvl, tril, n_levels, tt, chunk)
        x = _attn_call(x, kv, l, p, tt, final_norm=(l == depth - 1))
    return x
```

```python
import functools

import jax
import jax.numpy as jnp
import numpy as np
from jax import lax
from jax.experimental import pallas as pl
from jax.experimental.pallas import tpu as pltpu

EPS = 1e-6
HGRN_HEADS = 4
HGRN_KDIM = 128
POOL_WINDOWS = (2, 4, 8, 16)
POOL_CH = 128
POOL_HALO = 32
XA_HEADS = 4
SUBLANES = 8
MXU_DIM = 256

TIME_TILE = 512
HGRN_CHUNK = 128
HEADS_PER_DOT = MXU_DIM // HGRN_KDIM
VMEM_LIMIT_BYTES = 56 * 1024 * 1024

_BF16 = jnp.bfloat16
_F32 = jnp.float32


def _dot(a, b):
    return jnp.dot(a, b, preferred_element_type=_F32)


def _dot_nt(a, b):
    return lax.dot_general(a, b, (((1,), (1,)), ((), ())), preferred_element_type=_F32)


def _dot_tn(a, b):
    return lax.dot_general(a, b, (((0,), (0,)), ((), ())), preferred_element_type=_F32)


def _block_diag(blocks):
    n = len(blocks)
    zero = jnp.zeros_like(blocks[0])
    return jnp.concatenate(
        [jnp.concatenate([blocks[i] if i == j else zero for j in range(n)], axis=1) for i in range(n)], axis=0)


def _rmsnorm(x, g):
    ms = jnp.mean(x * x, axis=-1, keepdims=True)
    return x * lax.rsqrt(ms + EPS) * g


def _swiglu_half_step(x, g, w1_ref, w3_ref, w2_ref):
    hn = _rmsnorm(x, g).astype(_BF16)
    h1 = _dot(hn, w1_ref[...])
    h3 = _dot(hn, w3_ref[...])
    a = (h1 * jax.nn.sigmoid(h1) * h3).astype(_BF16)
    return x + 0.5 * _dot(a, w2_ref[...])


def _level_tables(chunk):
    n_levels = int(np.log2(chunk))
    t = np.arange(chunk)[:, None]
    s = np.arange(chunk)[None, :]
    x = np.bitwise_xor(t, s)
    hb = np.floor(np.log2(np.maximum(x, 1))).astype(np.int32)
    lvl = np.where(t > s, hb, np.where(t == s, n_levels, -1)).astype(np.int32)
    tril = (t >= s).astype(np.float32)
    return np.tile(lvl, (1, HEADS_PER_DOT)), tril, n_levels


def _boundary_rows(b, level_bit):
    c, w = b.shape
    m = 1 << level_bit
    assert m >= 2
    if 2 * m >= 2 * SUBLANES:
        nb = c // (2 * m)
        b3 = b.reshape(nb, 2 * m, w)
        return jnp.broadcast_to(b3[:, m - 1:m, :], (nb, 2 * m, w)).reshape(c, w)
    b3 = b.reshape(c // SUBLANES, SUBLANES, w)
    sub = lax.broadcasted_iota(jnp.int32, b3.shape, 1)

    def row(i):
        return jnp.broadcast_to(b3[:, i:i + 1, :], b3.shape)

    out = row(3) if m == 4 else jnp.where(sub < 4, row(1), row(5))
    return out.reshape(c, w)


def _hgrn_chunk(q, z, v, lb_rows, lvl, tril_bf, st_ref, n_levels):
    c, w = q.shape
    log_lb, log1m_lb, one_m_lb = lb_rows
    u = jnp.exp(-jnp.abs(z))
    log_sig = jnp.minimum(z, 0.0) - jnp.log1p(u)
    cterm = log1m_lb + log_sig
    mx = jnp.maximum(log_lb, cterm)
    g = mx + jnp.log1p(jnp.exp(-jnp.abs(log_lb - cterm)))
    kk = one_m_lb * (jnp.where(z >= 0.0, u, 1.0) / (1.0 + u))

    g_hi = g.astype(_BF16)
    r1 = g - g_hi.astype(_F32)
    g_mid = r1.astype(_BF16)
    g_lo = (r1 - g_mid.astype(_F32)).astype(_BF16)
    b = _dot(tril_bf, g_hi) + _dot(tril_bf, g_mid) + _dot(tril_bf, g_lo)

    q_bf = q.astype(_BF16)
    k_bf = kk.astype(_BF16)
    v_bf = v.astype(_BF16)
    qs = [q_bf]
    ks = [k_bf]
    odd = (lax.broadcasted_iota(jnp.int32, (c, w), 0) & 1) != 0
    for bit in range(n_levels):
        if bit == 0:
            e = jnp.where(odd, jnp.exp(g), 1.0).astype(_BF16)
        else:
            e = jnp.exp(-jnp.abs(b - _boundary_rows(b, bit))).astype(_BF16)
        qs.append(q_bf * e)
        ks.append(k_bf * e)
    level_ids = [n_levels] + list(range(n_levels))

    b_last = b[c - 1:c, :]
    q_in = (q * jnp.exp(b)).astype(_BF16)
    k_out = (kk * jnp.exp(b_last - b)).astype(_BF16)
    s_decay = jnp.exp(b_last)

    gw = HEADS_PER_DOT * HGRN_KDIM
    outs = []
    for grp in range(HGRN_HEADS // HEADS_PER_DOT):
        cols = slice(grp * gw, (grp + 1) * gw)
        heads = range(grp * HEADS_PER_DOT, (grp + 1) * HEADS_PER_DOT)
        hsl = [slice(h * HGRN_KDIM, (h + 1) * HGRN_KDIM) for h in heads]
        a = jnp.zeros((c, HEADS_PER_DOT * c), _F32)
        for lid, ql, kl in zip(level_ids, qs, ks):
            a = jnp.where(lvl == lid, _dot_nt(ql[:, cols], _block_diag([kl[:, s] for s in hsl])), a)
        st = [st_ref[h] for h in heads]
        o = (_dot(a.astype(_BF16), _block_diag([v_bf[:, s] for s in hsl]))
             + _dot_nt(q_in[:, cols], _block_diag([s.astype(_BF16) for s in st])))
        for h, s_old, s in zip(heads, st, hsl):
            st_ref[h] = s_old * s_decay[:, s] + _dot_tn(v_bf[:, s], k_out[:, s])
        outs.append(o)
    return outs


def _mixer_kernel(x_ref, g1_ref, w1_ref, w3_ref, w2_ref, gm_ref, win_ref, lb_ref, hn_ref, pw_ref, ps_ref,
                  wout_ref, lvl_ref, tril_ref, o_ref, st_ref, su_ref, sp_ref, sq_ref, mix_ref, *, chunk, n_levels):
    tt = x_ref.shape[0]
    d_a = HGRN_HEADS * HGRN_KDIM
    hal = POOL_HALO
    t_step = pl.program_id(1)

    @pl.when(t_step == 0)
    def _():
        st_ref[...] = jnp.zeros_like(st_ref)
        su_ref[0:hal, :] = jnp.zeros((hal, su_ref.shape[1]), _F32)

    x1 = _swiglu_half_step(x_ref[...], g1_ref[...], w1_ref, w3_ref, w2_ref)
    h = _rmsnorm(x1, gm_ref[...]).astype(_BF16)
    proj = _dot(h, win_ref[...])

    lb = lb_ref[...]
    lb_rows = (jnp.log(lb), jnp.log1p(-lb), 1.0 - lb)
    lvl = lvl_ref[...]
    tril_bf = tril_ref[...]
    hgrn_g = hn_ref[...]
    for ci in range(tt // chunk):
        rows = slice(ci * chunk, (ci + 1) * chunk)
        outs = _hgrn_chunk(proj[rows, 0:d_a], proj[rows, d_a:2 * d_a], proj[rows, 2 * d_a:3 * d_a],
                           lb_rows, lvl, tril_bf, st_ref, n_levels)
        for grp, og in enumerate(outs):
            for hi in range(HEADS_PER_DOT):
                hh = grp * HEADS_PER_DOT + hi
                sl = slice(hh * HGRN_KDIM, (hh + 1) * HGRN_KDIM)
                o = og[:, hi * HGRN_KDIM:(hi + 1) * HGRN_KDIM]
                o = o * lax.rsqrt(jnp.mean(o * o, axis=-1, keepdims=True) + EPS) * hgrn_g[:, sl]
                gate = proj[rows, 3 * d_a + hh * HGRN_KDIM:3 * d_a + (hh + 1) * HGRN_KDIM]
                mix_ref[rows, sl] = (o * (gate * jax.nn.sigmoid(gate))).astype(_BF16)

    u = proj[:, 4 * d_a:]
    n_rows = hal + tt
    su_ref[hal:n_rows, :] = u
    sp_ref[8:n_rows, :] = su_ref[8:n_rows, :] + su_ref[7:n_rows - 1, :]
    c1 = POOL_CH
    sq_ref[16:n_rows, c1:] = sp_ref[16:n_rows, c1:] + sp_ref[14:n_rows - 2, c1:]
    c2 = 2 * POOL_CH
    sp_ref[24:n_rows, c2:] = sq_ref[24:n_rows, c2:] + sq_ref[20:n_rows - 4, c2:]
    c3 = 3 * POOL_CH
    sums = [sp_ref[hal:n_rows, 0:c1], sq_ref[hal:n_rows, c1:c2], sp_ref[hal:n_rows, c2:c3],
            sp_ref[hal:n_rows, c3:] + sp_ref[hal - 8:n_rows - 8, c3:]]
    su_ref[0:hal, :] = u[tt - hal:tt, :]
    t_glob = t_step * tt + lax.broadcasted_iota(jnp.int32, (tt, 1), 0) + 1
    ps = []
    for j, win in enumerate(POOL_WINDOWS):
        cnt = jnp.minimum(t_glob, win).astype(_F32)
        ps.append((sums[j] / cnt - u[:, j * POOL_CH:(j + 1) * POOL_CH]).astype(_BF16))
    for grp in range(len(POOL_WINDOWS) // HEADS_PER_DOT):
        cols = slice(d_a + grp * MXU_DIM, d_a + (grp + 1) * MXU_DIM)
        p_grp = jnp.concatenate(ps[grp * HEADS_PER_DOT:(grp + 1) * HEADS_PER_DOT], axis=1)
        y = _dot(p_grp, pw_ref[grp]) * ps_ref[:, grp * MXU_DIM:(grp + 1) * MXU_DIM]
        mix_ref[:, cols] = y.astype(_BF16)

    o_ref[...] = x1 + _dot(mix_ref[...], wout_ref[...])


def _attn_kernel(x_ref, kv_ref, gx_ref, wq_ref, wo_ref, g2_ref, w1_ref, w3_ref, w2_ref, gf_ref, o_ref, att_ref,
                 *, final_norm):
    x2 = x_ref[...]
    d = x2.shape[1]
    hd = d // XA_HEADS
    h = _rmsnorm(x2, gx_ref[...]).astype(_BF16)
    q = (_dot(h, wq_ref[...]) * (hd ** -0.5)).astype(_BF16)
    for hh in range(XA_HEADS):
        k = kv_ref[:, hh * hd:(hh + 1) * hd]
        v = kv_ref[:, d + hh * hd:d + (hh + 1) * hd]
        s = _dot_nt(q[:, hh * hd:(hh + 1) * hd], k)
        s = s - jnp.max(s, axis=-1, keepdims=True)
        p = jnp.exp(s)
        p = p / jnp.sum(p, axis=-1, keepdims=True)
        att_ref[:, hh * hd:(hh + 1) * hd] = _dot(p.astype(_BF16), v).astype(_BF16)
    x3 = x2 + _dot(att_ref[...], wo_ref[...])
    x4 = _swiglu_half_step(x3, g2_ref[...], w1_ref, w3_ref, w2_ref)
    if final_norm:
        x4 = _rmsnorm(x4, gf_ref[...])
    o_ref[...] = x4


def _kv_kernel(mem_ref, g_ref, wkv_ref, o_ref):
    mn = _rmsnorm(mem_ref[...], g_ref[...]).astype(_BF16)
    o_ref[...] = _dot(mn, wkv_ref[...]).astype(_BF16)


def _resident(shape, index_map):
    return pl.BlockSpec(shape, index_map, pipeline_mode=pl.Buffered(1))


def _mixer_call(x, l, p, lvl, tril, n_levels, tt, chunk):
    bsz, t, d = x.shape
    f = p['ffn1_w1'].shape[2]
    d_in = p['w_in'].shape[2]
    d_a = HGRN_HEADS * HGRN_KDIM
    d_b = d_in - 4 * d_a
    n_pool_grp = len(POOL_WINDOWS) // HEADS_PER_DOT
    lsel = lambda *shape_tail: (lambda b, i: (l,) + (0,) * len(shape_tail))
    in_specs = [
        pl.BlockSpec((None, tt, d), lambda b, i: (b, i, 0)),
        _resident((None, 1, d), lsel(1, d)),
        _resident((None, d, f), lsel(d, f)),
        _resident((None, d, f), lsel(d, f)),
        _resident((None, f, d), lsel(f, d)),
        _resident((None, 1, d), lsel(1, d)),
        _resident((None, d, d_in), lsel(d, d_in)),
        _resident((None, 1, d_a), lsel(1, d_a)),
        _resident((None, 1, d_a), lsel(1, d_a)),
        _resident((None, n_pool_grp, MXU_DIM, MXU_DIM), lsel(n_pool_grp, MXU_DIM, MXU_DIM)),
        _resident((None, 1, d_b), lsel(1, d_b)),
        _resident((None, d, d), lsel(d, d)),
        _resident(lvl.shape, lambda b, i: (0, 0)),
        _resident(tril.shape, lambda b, i: (0, 0)),
    ]
    return pl.pallas_call(
        functools.partial(_mixer_kernel, chunk=chunk, n_levels=n_levels),
        grid=(bsz, t // tt),
        in_specs=in_specs,
        out_specs=pl.BlockSpec((None, tt, d), lambda b, i: (b, i, 0)),
        out_shape=jax.ShapeDtypeStruct(x.shape, _F32),
        scratch_shapes=[
            pltpu.VMEM((HGRN_HEADS, HGRN_KDIM, HGRN_KDIM), _F32),
            pltpu.VMEM((POOL_HALO + tt, d_b), _F32),
            pltpu.VMEM((POOL_HALO + tt, d_b), _F32),
            pltpu.VMEM((POOL_HALO + tt, d_b), _F32),
            pltpu.VMEM((tt, d), _BF16),
        ],
        compiler_params=pltpu.CompilerParams(
            dimension_semantics=("arbitrary", "arbitrary"), vmem_limit_bytes=VMEM_LIMIT_BYTES),
        name=f"mixer_l{l}",
    )(x, p['ffn1_norm'], p['ffn1_w1'], p['ffn1_w3'], p['ffn1_w2'], p['mix_norm'], p['w_in'], p['lb'],
      p['hgrn_norm'], p['pool_w'], p['pool_scale'], p['w_out'], lvl, tril)


def _attn_call(x, kv, l, p, tt, final_norm):
    bsz, t, d = x.shape
    f = p['ffn2_w1'].shape[2]
    n_mem = kv.shape[2]
    lsel = lambda *shape_tail: (lambda b, i: (l,) + (0,) * len(shape_tail))
    in_specs = [
        pl.BlockSpec((None, tt, d), lambda b, i: (b, i, 0)),
        pl.BlockSpec((None, None, n_mem, 2 * d), lambda b, i: (l, b, 0, 0)),
        _resident((None, 1, d), lsel(1, d)),
        _resident((None, d, d), lsel(d, d)),
        _resident((None, d, d), lsel(d, d)),
        _resident((None, 1, d), lsel(1, d)),
        _resident((None, d, f), lsel(d, f)),
        _resident((None, d, f), lsel(d, f)),
        _resident((None, f, d), lsel(f, d)),
        _resident((1, d), lambda b, i: (0, 0)),
    ]
    return pl.pallas_call(
        functools.partial(_attn_kernel, final_norm=final_norm),
        grid=(bsz, t // tt),
        in_specs=in_specs,
        out_specs=pl.BlockSpec((None, tt, d), lambda b, i: (b, i, 0)),
        out_shape=jax.ShapeDtypeStruct(x.shape, _F32),
        scratch_shapes=[pltpu.VMEM((tt, d), _BF16)],
        compiler_params=pltpu.CompilerParams(
            dimension_semantics=("arbitrary", "arbitrary"), vmem_limit_bytes=VMEM_LIMIT_BYTES),
        name=f"attn_l{l}",
    )(x, kv, p['xa_norm'], p['xa_wq'], p['xa_wo'], p['ffn2_norm'], p['ffn2_w1'], p['ffn2_w3'], p['ffn2_w2'],
      p['final_norm'])


def _kv_call(mem, mem_norm, wkv):
    bsz, n_mem, d = mem.shape
    depth = wkv.shape[0]
    return pl.pallas_call(
        _kv_kernel,
        grid=(depth, bsz),
        in_specs=[
            pl.BlockSpec((None, n_mem, d), lambda l, b: (b, 0, 0)),
            pl.BlockSpec((None, 1, d), lambda l, b: (l, 0, 0)),
            pl.BlockSpec((None, d, 2 * d), lambda l, b: (l, 0, 0)),
        ],
        out_specs=pl.BlockSpec((None, None, n_mem, 2 * d), lambda l, b: (l, b, 0, 0)),
        out_shape=jax.ShapeDtypeStruct((depth, bsz, n_mem, 2 * d), _BF16),
        compiler_params=pltpu.CompilerParams(
            dimension_semantics=("arbitrary", "arbitrary"), vmem_limit_bytes=VMEM_LIMIT_BYTES),
        name="mem_kv",
    )(mem, mem_norm, wkv)


def _pool_block_diag(pool_w):
    depth, n_grp, ch, _ = pool_w.shape
    w = pool_w.astype(_BF16).reshape(depth, n_grp // HEADS_PER_DOT, HEADS_PER_DOT, ch, ch)
    eye = jnp.eye(HEADS_PER_DOT, dtype=_BF16)
    out = w[:, :, :, :, None, :] * eye[None, None, :, None, :, None]
    return out.reshape(depth, n_grp // HEADS_PER_DOT, HEADS_PER_DOT * ch, HEADS_PER_DOT * ch)


def kernel(x, mem, ffn1_norm, ffn1_w1, ffn1_w3, ffn1_w2, mix_norm, w_in, lb_logits, hgrn_norm, pool_w, pool_scale,
           w_out, xa_norm, mem_norm, xa_wq, xa_wkv, xa_wo, ffn2_norm, ffn2_w1, ffn2_w3, ffn2_w2, final_norm):
    depth = w_in.shape[0]
    t = x.shape[1]
    tt = min(TIME_TILE, t)
    chunk = min(HGRN_CHUNK, tt)
    lvl_np, tril_np, n_levels = _level_tables(chunk)
    lvl = jnp.asarray(lvl_np)
    tril = jnp.asarray(tril_np, dtype=_BF16)

    lb_all = jnp.cumsum(jax.nn.softmax(lb_logits.astype(_F32), axis=0), axis=0)
    lb_all = lb_all - lb_all[0:1]

    row = lambda a: a.astype(_F32)[:, None, :]
    bf = lambda a: a.astype(_BF16)
    p = {
        'ffn1_norm': row(ffn1_norm), 'ffn1_w1': bf(ffn1_w1), 'ffn1_w3': bf(ffn1_w3), 'ffn1_w2': bf(ffn1_w2),
        'mix_norm': row(mix_norm), 'w_in': bf(w_in), 'lb': row(lb_all), 'hgrn_norm': row(hgrn_norm),
        'pool_w': _pool_block_diag(pool_w), 'pool_scale': row(pool_scale), 'w_out': bf(w_out),
        'xa_norm': row(xa_norm), 'xa_wq': bf(xa_wq), 'xa_wo': bf(xa_wo),
        'ffn2_norm': row(ffn2_norm), 'ffn2_w1': bf(ffn2_w1), 'ffn2_w3': bf(ffn2_w3), 'ffn2_w2': bf(ffn2_w2),
        'final_norm': final_norm.astype(_F32)[None, :],
    }
    kv = _kv_call(mem, row(mem_norm), bf(xa_wkv))
    for l in range(depth):
        x = _mixer_call(x, l, p, lvl, tril, n_levels, tt, chunk)
        x = _attn_call(x, kv, l, p, tt, final_norm=(l == depth - 1))
    return x
```

```python
import functools

import jax
import jax.numpy as jnp
import numpy as np
from jax import lax
from jax.experimental import pallas as pl
from jax.experimental.pallas import tpu as pltpu

EPS = 1e-6
HGRN_HEADS = 4
HGRN_KDIM = 128
POOL_WINDOWS = (2, 4, 8, 16)
POOL_CH = 128
POOL_HALO = 32
XA_HEADS = 4
SUBLANES = 8
MXU_DIM = 256

TIME_TILE = 512
HGRN_CHUNK = 128
HEADS_PER_DOT = MXU_DIM // HGRN_KDIM
VMEM_LIMIT_BYTES = 56 * 1024 * 1024

_BF16 = jnp.bfloat16
_F32 = jnp.float32


def _dot(a, b):
    return jnp.dot(a, b, preferred_element_type=_F32)


def _dot_nt(a, b):
    return lax.dot_general(a, b, (((1,), (1,)), ((), ())), preferred_element_type=_F32)


def _dot_tn(a, b):
    return lax.dot_general(a, b, (((0,), (0,)), ((), ())), preferred_element_type=_F32)


def _block_diag(blocks):
    n = len(blocks)
    zero = jnp.zeros_like(blocks[0])
    return jnp.concatenate(
        [jnp.concatenate([blocks[i] if i == j else zero for j in range(n)], axis=1) for i in range(n)], axis=0)


def _rmsnorm(x, g):
    ms = jnp.mean(x * x, axis=-1, keepdims=True)
    return x * lax.rsqrt(ms + EPS) * g


def _swiglu_half_step(x, g, w1_ref, w3_ref, w2_ref):
    hn = _rmsnorm(x, g).astype(_BF16)
    h1 = _dot(hn, w1_ref[...])
    h3 = _dot(hn, w3_ref[...])
    a = (h1 * jax.nn.sigmoid(h1) * h3).astype(_BF16)
    return x + 0.5 * _dot(a, w2_ref[...])


def _level_tables(chunk):
    n_levels = int(np.log2(chunk))
    t = np.arange(chunk)[:, None]
    s = np.arange(chunk)[None, :]
    x = np.bitwise_xor(t, s)
    hb = np.floor(np.log2(np.maximum(x, 1))).astype(np.int32)
    lvl = np.where(t > s, hb, np.where(t == s, n_levels, -1)).astype(np.int32)
    tril = (t >= s).astype(np.float32)
    return np.tile(lvl, (1, HEADS_PER_DOT)), tril, n_levels


def _boundary_rows(b, level_bit):
    c, w = b.shape
    m = 1 << level_bit
    assert m >= 2
    if 2 * m >= 2 * SUBLANES:
        nb = c // (2 * m)
        b3 = b.reshape(nb, 2 * m, w)
        return jnp.broadcast_to(b3[:, m - 1:m, :], (nb, 2 * m, w)).reshape(c, w)
    b3 = b.reshape(c // SUBLANES, SUBLANES, w)
    sub = lax.broadcasted_iota(jnp.int32, b3.shape, 1)

    def row(i):
        return jnp.broadcast_to(b3[:, i:i + 1, :], b3.shape)

    out = row(3) if m == 4 else jnp.where(sub < 4, row(1), row(5))
    return out.reshape(c, w)


def _hgrn_chunk(q, z, v, lb_rows, lvl, tril_bf, st_ref, n_levels):
    c, w = q.shape
    log_lb, log1m_lb, one_m_lb = lb_rows
    u = jnp.exp(-jnp.abs(z))
    one_p_u = 1.0 + u
    log_sig = jnp.minimum(z, 0.0) - jnp.log(one_p_u)
    cterm = log1m_lb + log_sig
    mx = jnp.maximum(log_lb, cterm)
    g = mx + jnp.log(1.0 + jnp.exp(-jnp.abs(log_lb - cterm)))
    kk = one_m_lb * (jnp.where(z >= 0.0, u, 1.0) / one_p_u)

    g_hi = g.astype(_BF16)
    r1 = g - g_hi.astype(_F32)
    g_mid = r1.astype(_BF16)
    g_lo = (r1 - g_mid.astype(_F32)).astype(_BF16)
    b = _dot(tril_bf, g_hi) + _dot(tril_bf, g_mid) + _dot(tril_bf, g_lo)

    q_bf = q.astype(_BF16)
    k_bf = kk.astype(_BF16)
    v_bf = v.astype(_BF16)
    qs = [q_bf]
    ks = [k_bf]
    odd = (lax.broadcasted_iota(jnp.int32, (c, w), 0) & 1) != 0
    for bit in range(n_levels):
        if bit == 0:
            e = jnp.where(odd, jnp.exp(g), 1.0).astype(_BF16)
        else:
            e = jnp.exp(-jnp.abs(b - _boundary_rows(b, bit))).astype(_BF16)
        qs.append(q_bf * e)
        ks.append(k_bf * e)
    level_ids = [n_levels] + list(range(n_levels))

    b_last = b[c - 1:c, :]
    q_in = (q * jnp.exp(b)).astype(_BF16)
    k_out = (kk * jnp.exp(b_last - b)).astype(_BF16)
    s_decay = jnp.exp(b_last)

    gw = HEADS_PER_DOT * HGRN_KDIM
    outs = []
    for grp in range(HGRN_HEADS // HEADS_PER_DOT):
        cols = slice(grp * gw, (grp + 1) * gw)
        heads = range(grp * HEADS_PER_DOT, (grp + 1) * HEADS_PER_DOT)
        hsl = [slice(h * HGRN_KDIM, (h + 1) * HGRN_KDIM) for h in heads]
        a = jnp.zeros((c, HEADS_PER_DOT * c), _F32)
        for lid, ql, kl in zip(level_ids, qs, ks):
            a = jnp.where(lvl == lid, _dot_nt(ql[:, cols], _block_diag([kl[:, s] for s in hsl])), a)
        st = [st_ref[h] for h in heads]
        o = (_dot(a.astype(_BF16), _block_diag([v_bf[:, s] for s in hsl]))
             + _dot_nt(q_in[:, cols], _block_diag([s.astype(_BF16) for s in st])))
        for h, s_old, s in zip(heads, st, hsl):
            st_ref[h] = s_old * s_decay[:, s] + _dot_tn(v_bf[:, s], k_out[:, s])
        outs.append(o)
    return outs


def _mixer_kernel(x_ref, g1_ref, w1_ref, w3_ref, w2_ref, gm_ref, win_ref, lb_ref, hn_ref, pw_ref, ps_ref,
                  wout_ref, lvl_ref, tril_ref, o_ref, st_ref, su_ref, sp_ref, sq_ref, mix_ref, *, chunk, n_levels):
    tt = x_ref.shape[0]
    d_a = HGRN_HEADS * HGRN_KDIM
    hal = POOL_HALO
    t_step = pl.program_id(1)

    @pl.when(t_step == 0)
    def _():
        st_ref[...] = jnp.zeros_like(st_ref)
        su_ref[0:hal, :] = jnp.zeros((hal, su_ref.shape[1]), _F32)

    x1 = _swiglu_half_step(x_ref[...], g1_ref[...], w1_ref, w3_ref, w2_ref)
    h = _rmsnorm(x1, gm_ref[...]).astype(_BF16)
    proj = _dot(h, win_ref[...])

    lb = lb_ref[...]
    lb_rows = (jnp.log(lb), jnp.log1p(-lb), 1.0 - lb)
    lvl = lvl_ref[...]
    tril_bf = tril_ref[...]
    hgrn_g = hn_ref[...]
    for ci in range(tt // chunk):
        rows = slice(ci * chunk, (ci + 1) * chunk)
        outs = _hgrn_chunk(proj[rows, 0:d_a], proj[rows, d_a:2 * d_a], proj[rows, 2 * d_a:3 * d_a],
                           lb_rows, lvl, tril_bf, st_ref, n_levels)
        for grp, og in enumerate(outs):
            for hi in range(HEADS_PER_DOT):
                hh = grp * HEADS_PER_DOT + hi
                sl = slice(hh * HGRN_KDIM, (hh + 1) * HGRN_KDIM)
                o = og[:, hi * HGRN_KDIM:(hi + 1) * HGRN_KDIM]
                o = o * lax.rsqrt(jnp.mean(o * o, axis=-1, keepdims=True) + EPS) * hgrn_g[:, sl]
                gate = proj[rows, 3 * d_a + hh * HGRN_KDIM:3 * d_a + (hh + 1) * HGRN_KDIM]
                mix_ref[rows, sl] = (o * (gate * jax.nn.sigmoid(gate))).astype(_BF16)

    u = proj[:, 4 * d_a:]
    n_rows = hal + tt
    su_ref[hal:n_rows, :] = u
    sp_ref[8:n_rows, :] = su_ref[8:n_rows, :] + su_ref[7:n_rows - 1, :]
    c1 = POOL_CH
    sq_ref[16:n_rows, c1:] = sp_ref[16:n_rows, c1:] + sp_ref[14:n_rows - 2, c1:]
    c2 = 2 * POOL_CH
    sp_ref[24:n_rows, c2:] = sq_ref[24:n_rows, c2:] + sq_ref[20:n_rows - 4, c2:]
    c3 = 3 * POOL_CH
    sums = [sp_ref[hal:n_rows, 0:c1], sq_ref[hal:n_rows, c1:c2], sp_ref[hal:n_rows, c2:c3],
            sp_ref[hal:n_rows, c3:] + sp_ref[hal - 8:n_rows - 8, c3:]]
    su_ref[0:hal, :] = u[tt - hal:tt, :]
    t_glob = t_step * tt + lax.broadcasted_iota(jnp.int32, (tt, 1), 0) + 1
    ps = []
    for j, win in enumerate(POOL_WINDOWS):
        cnt = jnp.minimum(t_glob, win).astype(_F32)
        ps.append((sums[j] / cnt - u[:, j * POOL_CH:(j + 1) * POOL_CH]).astype(_BF16))
    for grp in range(len(POOL_WINDOWS) // HEADS_PER_DOT):
        cols = slice(d_a + grp * MXU_DIM, d_a + (grp + 1) * MXU_DIM)
        p_grp = jnp.concatenate(ps[grp * HEADS_PER_DOT:(grp + 1) * HEADS_PER_DOT], axis=1)
        y = _dot(p_grp, pw_ref[grp]) * ps_ref[:, grp * MXU_DIM:(grp + 1) * MXU_DIM]
        mix_ref[:, cols] = y.astype(_BF16)

    o_ref[...] = x1 + _dot(mix_ref[...], wout_ref[...])


def _attn_kernel(x_ref, kv_ref, gx_ref, wq_ref, wo_ref, g2_ref, w1_ref, w3_ref, w2_ref, gf_ref, o_ref, att_ref,
                 *, final_norm):
    x2 = x_ref[...]
    d = x2.shape[1]
    hd = d // XA_HEADS
    h = _rmsnorm(x2, gx_ref[...]).astype(_BF16)
    q = (_dot(h, wq_ref[...]) * (hd ** -0.5)).astype(_BF16)
    scores = [_dot_nt(q[:, hh * hd:(hh + 1) * hd], kv_ref[:, hh * hd:(hh + 1) * hd]) for hh in range(XA_HEADS)]
    probs = []
    for s in scores:
        p = jnp.exp(s - jnp.max(s, axis=-1, keepdims=True))
        probs.append((p / jnp.sum(p, axis=-1, keepdims=True)).astype(_BF16))
    for hh, p in enumerate(probs):
        v = kv_ref[:, d + hh * hd:d + (hh + 1) * hd]
        att_ref[:, hh * hd:(hh + 1) * hd] = _dot(p, v).astype(_BF16)
    x3 = x2 + _dot(att_ref[...], wo_ref[...])
    x4 = _swiglu_half_step(x3, g2_ref[...], w1_ref, w3_ref, w2_ref)
    if final_norm:
        x4 = _rmsnorm(x4, gf_ref[...])
    o_ref[...] = x4


def _kv_kernel(mem_ref, g_ref, wkv_ref, o_ref):
    mn = _rmsnorm(mem_ref[...], g_ref[...]).astype(_BF16)
    o_ref[...] = _dot(mn, wkv_ref[...]).astype(_BF16)


def _resident(shape, index_map):
    return pl.BlockSpec(shape, index_map, pipeline_mode=pl.Buffered(1))


def _mixer_call(x, l, p, lvl, tril, n_levels, tt, chunk):
    bsz, t, d = x.shape
    f = p['ffn1_w1'].shape[2]
    d_in = p['w_in'].shape[2]
    d_a = HGRN_HEADS * HGRN_KDIM
    d_b = d_in - 4 * d_a
    n_pool_grp = len(POOL_WINDOWS) // HEADS_PER_DOT
    lsel = lambda *shape_tail: (lambda b, i: (l,) + (0,) * len(shape_tail))
    in_specs = [
        pl.BlockSpec((None, tt, d), lambda b, i: (b, i, 0)),
        _resident((None, 1, d), lsel(1, d)),
        _resident((None, d, f), lsel(d, f)),
        _resident((None, d, f), lsel(d, f)),
        _resident((None, f, d), lsel(f, d)),
        _resident((None, 1, d), lsel(1, d)),
        _resident((None, d, d_in), lsel(d, d_in)),
        _resident((None, 1, d_a), lsel(1, d_a)),
        _resident((None, 1, d_a), lsel(1, d_a)),
        _resident((None, n_pool_grp, MXU_DIM, MXU_DIM), lsel(n_pool_grp, MXU_DIM, MXU_DIM)),
        _resident((None, 1, d_b), lsel(1, d_b)),
        _resident((None, d, d), lsel(d, d)),
        _resident(lvl.shape, lambda b, i: (0, 0)),
        _resident(tril.shape, lambda b, i: (0, 0)),
    ]
    return pl.pallas_call(
        functools.partial(_mixer_kernel, chunk=chunk, n_levels=n_levels),
        grid=(bsz, t // tt),
        in_specs=in_specs,
        out_specs=pl.BlockSpec((None, tt, d), lambda b, i: (b, i, 0)),
        out_shape=jax.ShapeDtypeStruct(x.shape, _F32),
        scratch_shapes=[
            pltpu.VMEM((HGRN_HEADS, HGRN_KDIM, HGRN_KDIM), _F32),
            pltpu.VMEM((POOL_HALO + tt, d_b), _F32),
            pltpu.VMEM((POOL_HALO + tt, d_b), _F32),
            pltpu.VMEM((POOL_HALO + tt, d_b), _F32),
            pltpu.VMEM((tt, d), _BF16),
        ],
        compiler_params=pltpu.CompilerParams(
            dimension_semantics=("arbitrary", "arbitrary"), vmem_limit_bytes=VMEM_LIMIT_BYTES),
        name=f"mixer_l{l}",
    )(x, p['ffn1_norm'], p['ffn1_w1'], p['ffn1_w3'], p['ffn1_w2'], p['mix_norm'], p['w_in'], p['lb'],
      p['hgrn_norm'], p['pool_w'], p['pool_scale'], p['w_out'], lvl, tril)


def _attn_call(x, kv, l, p, tt, final_norm):
    bsz, t, d = x.shape
    f = p['ffn2_w1'].shape[2]
    n_mem = kv.shape[2]
    lsel = lambda *shape_tail: (lambda b, i: (l,) + (0,) * len(shape_tail))
    in_specs = [
        pl.BlockSpec((None, tt, d), lambda b, i: (b, i, 0)),
        pl.BlockSpec((None, None, n_mem, 2 * d), lambda b, i: (l, b, 0, 0)),
        _resident((None, 1, d), lsel(1, d)),
        _resident((None, d, d), lsel(d, d)),
        _resident((None, d, d), lsel(d, d)),
        _resident((None, 1, d), lsel(1, d)),
        _resident((None, d, f), lsel(d, f)),
        _resident((None, d, f), lsel(d, f)),
        _resident((None, f, d), lsel(f, d)),
        _resident((1, d), lambda b, i: (0, 0)),
    ]
    return pl.pallas_call(
        functools.partial(_attn_kernel, final_norm=final_norm),
        grid=(bsz, t // tt),
        in_specs=in_specs,
        out_specs=pl.BlockSpec((None, tt, d), lambda b, i: (b, i, 0)),
        out_shape=jax.ShapeDtypeStruct(x.shape, _F32),
        scratch_shapes=[pltpu.VMEM((tt, d), _BF16)],
        compiler_params=pltpu.CompilerParams(
            dimension_semantics=("arbitrary", "arbitrary"), vmem_limit_bytes=VMEM_LIMIT_BYTES),
        name=f"attn_l{l}",
    )(x, kv, p['xa_norm'], p['xa_wq'], p['xa_wo'], p['ffn2_norm'], p['ffn2_w1'], p['ffn2_w3'], p['ffn2_w2'],
      p['final_norm'])


def _kv_call(mem, mem_norm, wkv):
    bsz, n_mem, d = mem.shape
    depth = wkv.shape[0]
    return pl.pallas_call(
        _kv_kernel,
        grid=(depth, bsz),
        in_specs=[
            pl.BlockSpec((None, n_mem, d), lambda l, b: (b, 0, 0)),
            pl.BlockSpec((None, 1, d), lambda l, b: (l, 0, 0)),
            pl.BlockSpec((None, d, 2 * d), lambda l, b: (l, 0, 0)),
        ],
        out_specs=pl.BlockSpec((None, None, n_mem, 2 * d), lambda l, b: (l, b, 0, 0)),
        out_shape=jax.ShapeDtypeStruct((depth, bsz, n_mem, 2 * d), _BF16),
        compiler_params=pltpu.CompilerParams(
            dimension_semantics=("arbitrary", "arbitrary"), vmem_limit_bytes=VMEM_LIMIT_BYTES),
        name="mem_kv",
    )(mem, mem_norm, wkv)


def _pool_block_diag(pool_w):
    depth, n_grp, ch, _ = pool_w.shape
    w = pool_w.astype(_BF16).reshape(depth, n_grp // HEADS_PER_DOT, HEADS_PER_DOT, ch, ch)
    eye = jnp.eye(HEADS_PER_DOT, dtype=_BF16)
    out = w[:, :, :, :, None, :] * eye[None, None, :, None, :, None]
    return out.reshape(depth, n_grp // HEADS_PER_DOT, HEADS_PER_DOT * ch, HEADS_PER_DOT * ch)


def kernel(x, mem, ffn1_norm, ffn1_w1, ffn1_w3, ffn1_w2, mix_norm, w_in, lb_logits, hgrn_norm, pool_w, pool_scale,
           w_out, xa_norm, mem_norm, xa_wq, xa_wkv, xa_wo, ffn2_norm, ffn2_w1, ffn2_w3, ffn2_w2, final_norm):
    depth = w_in.shape[0]
    t = x.shape[1]
    tt = min(TIME_TILE, t)
    chunk = min(HGRN_CHUNK, tt)
    lvl_np, tril_np, n_levels = _level_tables(chunk)
    lvl = jnp.asarray(lvl_np)
    tril = jnp.asarray(tril_np, dtype=_BF16)

    lb_all = jnp.cumsum(jax.nn.softmax(lb_logits.astype(_F32), axis=0), axis=0)
    lb_all = lb_all - lb_all[0:1]

    row = lambda a: a.astype(_F32)[:, None, :]
    bf = lambda a: a.astype(_BF16)
    p = {
        'ffn1_norm': row(ffn1_norm), 'ffn1_w1': bf(ffn1_w1), 'ffn1_w3': bf(ffn1_w3), 'ffn1_w2': bf(ffn1_w2),
        'mix_norm': row(mix_norm), 'w_in': bf(w_in), 'lb': row(lb_all), 'hgrn_norm': row(hgrn_norm),
        'pool_w': _pool_block_diag(pool_w), 'pool_scale': row(pool_scale), 'w_out': bf(w_out),
        'xa_norm': row(xa_norm), 'xa_wq': bf(xa_wq), 'xa_wo': bf(xa_wo),
        'ffn2_norm': row(ffn2_norm), 'ffn2_w1': bf(ffn2_w1), 'ffn2_w3': bf(ffn2_w3), 'ffn2_w2': bf(ffn2_w2),
        'final_norm': final_norm.astype(_F32)[None, :],
    }
    kv = _kv_call(mem, row(mem_norm), bf(xa_wkv))
    for l in range(depth):
        x = _mixer_call(x, l, p, lvl, tril, n_levels, tt, chunk)
        x = _attn_call(x, kv, l, p, tt, final_norm=(l == depth - 1))
    return x
```

```python
import functools

import jax
import jax.numpy as jnp
import numpy as np
from jax import lax
from jax.experimental import pallas as pl
from jax.experimental.pallas import tpu as pltpu

EPS = 1e-6
HGRN_HEADS = 4
HGRN_KDIM = 128
POOL_WINDOWS = (2, 4, 8, 16)
POOL_CH = 128
POOL_HALO = 32
XA_HEADS = 4
SUBLANES = 8
MXU_DIM = 256

TIME_TILE = 512
ATTN_TIME_TILE = 1024
HGRN_CHUNK = 128
HEADS_PER_DOT = MXU_DIM // HGRN_KDIM
VMEM_LIMIT_BYTES = 56 * 1024 * 1024
ATTN_VMEM_LIMIT_BYTES = 60 * 1024 * 1024

_BF16 = jnp.bfloat16
_F32 = jnp.float32


def _dot(a, b):
    return jnp.dot(a, b, preferred_element_type=_F32)


def _dot_nt(a, b):
    return lax.dot_general(a, b, (((1,), (1,)), ((), ())), preferred_element_type=_F32)


def _dot_tn(a, b):
    return lax.dot_general(a, b, (((0,), (0,)), ((), ())), preferred_element_type=_F32)


def _block_diag(blocks):
    n = len(blocks)
    zero = jnp.zeros_like(blocks[0])
    return jnp.concatenate(
        [jnp.concatenate([blocks[i] if i == j else zero for j in range(n)], axis=1) for i in range(n)], axis=0)


def _rmsnorm(x, g):
    ms = jnp.mean(x * x, axis=-1, keepdims=True)
    return x * lax.rsqrt(ms + EPS) * g


def _swiglu_half_step(x, g, w1_ref, w3_ref, w2_ref):
    hn = _rmsnorm(x, g).astype(_BF16)
    h1 = _dot(hn, w1_ref[...])
    h3 = _dot(hn, w3_ref[...])
    a = (h1 * jax.nn.sigmoid(h1) * h3).astype(_BF16)
    return x + 0.5 * _dot(a, w2_ref[...])


def _level_tables(chunk):
    n_levels = int(np.log2(chunk))
    t = np.arange(chunk)[:, None]
    s = np.arange(chunk)[None, :]
    x = np.bitwise_xor(t, s)
    hb = np.floor(np.log2(np.maximum(x, 1))).astype(np.int32)
    lvl = np.where(t > s, hb, np.where(t == s, n_levels, -1)).astype(np.int32)
    tril = (t >= s).astype(np.float32)
    return np.tile(lvl, (1, HEADS_PER_DOT)), tril, n_levels


def _boundary_rows(b, level_bit):
    c, w = b.shape
    m = 1 << level_bit
    assert m >= 2
    if 2 * m >= 2 * SUBLANES:
        nb = c // (2 * m)
        b3 = b.reshape(nb, 2 * m, w)
        return jnp.broadcast_to(b3[:, m - 1:m, :], (nb, 2 * m, w)).reshape(c, w)
    b3 = b.reshape(c // SUBLANES, SUBLANES, w)
    sub = lax.broadcasted_iota(jnp.int32, b3.shape, 1)

    def row(i):
        return jnp.broadcast_to(b3[:, i:i + 1, :], b3.shape)

    out = row(3) if m == 4 else jnp.where(sub < 4, row(1), row(5))
    return out.reshape(c, w)


def _hgrn_chunk(q, z, v, lb_rows, lvl, tril_bf, st_ref, n_levels):
    c, w = q.shape
    log_lb, log1m_lb, one_m_lb = lb_rows
    u = jnp.exp(-jnp.abs(z))
    one_p_u = 1.0 + u
    log_sig = jnp.minimum(z, 0.0) - jnp.log(one_p_u)
    cterm = log1m_lb + log_sig
    mx = jnp.maximum(log_lb, cterm)
    g = mx + jnp.log(1.0 + jnp.exp(-jnp.abs(log_lb - cterm)))
    kk = one_m_lb * (jnp.where(z >= 0.0, u, 1.0) / one_p_u)

    g_hi = g.astype(_BF16)
    r1 = g - g_hi.astype(_F32)
    g_mid = r1.astype(_BF16)
    g_lo = (r1 - g_mid.astype(_F32)).astype(_BF16)
    b = _dot(tril_bf, g_hi) + _dot(tril_bf, g_mid) + _dot(tril_bf, g_lo)

    q_bf = q.astype(_BF16)
    k_bf = kk.astype(_BF16)
    v_bf = v.astype(_BF16)
    qs = [q_bf]
    ks = [k_bf]
    odd = (lax.broadcasted_iota(jnp.int32, (c, w), 0) & 1) != 0
    for bit in range(n_levels):
        if bit == 0:
            e = jnp.where(odd, jnp.exp(g), 1.0).astype(_BF16)
        else:
            e = jnp.exp(-jnp.abs(b - _boundary_rows(b, bit))).astype(_BF16)
        qs.append(q_bf * e)
        ks.append(k_bf * e)
    level_ids = [n_levels] + list(range(n_levels))

    b_last = b[c - 1:c, :]
    q_in = (q * jnp.exp(b)).astype(_BF16)
    k_out = (kk * jnp.exp(b_last - b)).astype(_BF16)
    s_decay = jnp.exp(b_last)

    gw = HEADS_PER_DOT * HGRN_KDIM
    outs = []
    for grp in range(HGRN_HEADS // HEADS_PER_DOT):
        cols = slice(grp * gw, (grp + 1) * gw)
        heads = range(grp * HEADS_PER_DOT, (grp + 1) * HEADS_PER_DOT)
        hsl = [slice(h * HGRN_KDIM, (h + 1) * HGRN_KDIM) for h in heads]
        a = jnp.zeros((c, HEADS_PER_DOT * c), _F32)
        for lid, ql, kl in zip(level_ids, qs, ks):
            a = jnp.where(lvl == lid, _dot_nt(ql[:, cols], _block_diag([kl[:, s] for s in hsl])), a)
        st = [st_ref[h] for h in heads]
        o = (_dot(a.astype(_BF16), _block_diag([v_bf[:, s] for s in hsl]))
             + _dot_nt(q_in[:, cols], _block_diag([s.astype(_BF16) for s in st])))
        for h, s_old, s in zip(heads, st, hsl):
            st_ref[h] = s_old * s_decay[:, s] + _dot_tn(v_bf[:, s], k_out[:, s])
        outs.append(o)
    return outs


def _mixer_kernel(x_ref, g1_ref, w1_ref, w3_ref, w2_ref, gm_ref, win_ref, lb_ref, hn_ref, pw_ref, ps_ref,
                  wout_ref, lvl_ref, tril_ref, o_ref, st_ref, su_ref, sp_ref, sq_ref, mix_ref, *, chunk, n_levels):
    tt = x_ref.shape[0]
    d_a = HGRN_HEADS * HGRN_KDIM
    hal = POOL_HALO
    t_step = pl.program_id(1)

    @pl.when(t_step == 0)
    def _():
        st_ref[...] = jnp.zeros_like(st_ref)
        su_ref[0:hal, :] = jnp.zeros((hal, su_ref.shape[1]), _F32)

    x1 = _swiglu_half_step(x_ref[...], g1_ref[...], w1_ref, w3_ref, w2_ref)
    h = _rmsnorm(x1, gm_ref[...]).astype(_BF16)
    proj = _dot(h, win_ref[...])

    lb = lb_ref[...]
    lb_rows = (jnp.log(lb), jnp.log1p(-lb), 1.0 - lb)
    lvl = lvl_ref[...]
    tril_bf = tril_ref[...]
    hgrn_g = hn_ref[...]
    for ci in range(tt // chunk):
        rows = slice(ci * chunk, (ci + 1) * chunk)
        outs = _hgrn_chunk(proj[rows, 0:d_a], proj[rows, d_a:2 * d_a], proj[rows, 2 * d_a:3 * d_a],
                           lb_rows, lvl, tril_bf, st_ref, n_levels)
        for grp, og in enumerate(outs):
            for hi in range(HEADS_PER_DOT):
                hh = grp * HEADS_PER_DOT + hi
                sl = slice(hh * HGRN_KDIM, (hh + 1) * HGRN_KDIM)
                o = og[:, hi * HGRN_KDIM:(hi + 1) * HGRN_KDIM]
                o = o * lax.rsqrt(jnp.mean(o * o, axis=-1, keepdims=True) + EPS) * hgrn_g[:, sl]
                gate = proj[rows, 3 * d_a + hh * HGRN_KDIM:3 * d_a + (hh + 1) * HGRN_KDIM]
                mix_ref[rows, sl] = (o * (gate * jax.nn.sigmoid(gate))).astype(_BF16)

    u = proj[:, 4 * d_a:]
    n_rows = hal + tt
    su_ref[hal:n_rows, :] = u
    sp_ref[8:n_rows, :] = su_ref[8:n_rows, :] + su_ref[7:n_rows - 1, :]
    c1 = POOL_CH
    sq_ref[16:n_rows, c1:] = sp_ref[16:n_rows, c1:] + sp_ref[14:n_rows - 2, c1:]
    c2 = 2 * POOL_CH
    sp_ref[24:n_rows, c2:] = sq_ref[24:n_rows, c2:] + sq_ref[20:n_rows - 4, c2:]
    c3 = 3 * POOL_CH
    sums = [sp_ref[hal:n_rows, 0:c1], sq_ref[hal:n_rows, c1:c2], sp_ref[hal:n_rows, c2:c3],
            sp_ref[hal:n_rows, c3:] + sp_ref[hal - 8:n_rows - 8, c3:]]
    su_ref[0:hal, :] = u[tt - hal:tt, :]
    t_glob = t_step * tt + lax.broadcasted_iota(jnp.int32, (tt, 1), 0) + 1
    ps = []
    for j, win in enumerate(POOL_WINDOWS):
        cnt = jnp.minimum(t_glob, win).astype(_F32)
        ps.append((sums[j] / cnt - u[:, j * POOL_CH:(j + 1) * POOL_CH]).astype(_BF16))
    for grp in range(len(POOL_WINDOWS) // HEADS_PER_DOT):
        cols = slice(d_a + grp * MXU_DIM, d_a + (grp + 1) * MXU_DIM)
        p_grp = jnp.concatenate(ps[grp * HEADS_PER_DOT:(grp + 1) * HEADS_PER_DOT], axis=1)
        y = _dot(p_grp, pw_ref[grp]) * ps_ref[:, grp * MXU_DIM:(grp + 1) * MXU_DIM]
        mix_ref[:, cols] = y.astype(_BF16)

    o_ref[...] = x1 + _dot(mix_ref[...], wout_ref[...])


def _attn_kernel(x_ref, kv_ref, gx_ref, wq_ref, wo_ref, g2_ref, w1_ref, w3_ref, w2_ref, gf_ref, o_ref, att_ref,
                 *, final_norm):
    x2 = x_ref[...]
    d = x2.shape[1]
    hd = d // XA_HEADS
    h = _rmsnorm(x2, gx_ref[...]).astype(_BF16)
    q = (_dot(h, wq_ref[...]) * (hd ** -0.5)).astype(_BF16)
    scores = [_dot_nt(q[:, hh * hd:(hh + 1) * hd], kv_ref[:, hh * hd:(hh + 1) * hd]) for hh in range(XA_HEADS)]
    probs = []
    for s in scores:
        p = jnp.exp(s - jnp.max(s, axis=-1, keepdims=True))
        probs.append((p / jnp.sum(p, axis=-1, keepdims=True)).astype(_BF16))
    for hh, p in enumerate(probs):
        v = kv_ref[:, d + hh * hd:d + (hh + 1) * hd]
        att_ref[:, hh * hd:(hh + 1) * hd] = _dot(p, v).astype(_BF16)
    x3 = x2 + _dot(att_ref[...], wo_ref[...])
    x4 = _swiglu_half_step(x3, g2_ref[...], w1_ref, w3_ref, w2_ref)
    if final_norm:
        x4 = _rmsnorm(x4, gf_ref[...])
    o_ref[...] = x4


def _kv_kernel(mem_ref, g_ref, wkv_ref, o_ref):
    mn = _rmsnorm(mem_ref[...], g_ref[...]).astype(_BF16)
    o_ref[...] = _dot(mn, wkv_ref[...]).astype(_BF16)


def _resident(shape, index_map):
    return pl.BlockSpec(shape, index_map, pipeline_mode=pl.Buffered(1))


def _mixer_call(x, l, p, lvl, tril, n_levels, tt, chunk):
    bsz, t, d = x.shape
    f = p['ffn1_w1'].shape[2]
    d_in = p['w_in'].shape[2]
    d_a = HGRN_HEADS * HGRN_KDIM
    d_b = d_in - 4 * d_a
    n_pool_grp = len(POOL_WINDOWS) // HEADS_PER_DOT
    lsel = lambda *shape_tail: (lambda b, i: (l,) + (0,) * len(shape_tail))
    in_specs = [
        pl.BlockSpec((None, tt, d), lambda b, i: (b, i, 0)),
        _resident((None, 1, d), lsel(1, d)),
        _resident((None, d, f), lsel(d, f)),
        _resident((None, d, f), lsel(d, f)),
        _resident((None, f, d), lsel(f, d)),
        _resident((None, 1, d), lsel(1, d)),
        _resident((None, d, d_in), lsel(d, d_in)),
        _resident((None, 1, d_a), lsel(1, d_a)),
        _resident((None, 1, d_a), lsel(1, d_a)),
        _resident((None, n_pool_grp, MXU_DIM, MXU_DIM), lsel(n_pool_grp, MXU_DIM, MXU_DIM)),
        _resident((None, 1, d_b), lsel(1, d_b)),
        _resident((None, d, d), lsel(d, d)),
        _resident(lvl.shape, lambda b, i: (0, 0)),
        _resident(tril.shape, lambda b, i: (0, 0)),
    ]
    return pl.pallas_call(
        functools.partial(_mixer_kernel, chunk=chunk, n_levels=n_levels),
        grid=(bsz, t // tt),
        in_specs=in_specs,
        out_specs=pl.BlockSpec((None, tt, d), lambda b, i: (b, i, 0)),
        out_shape=jax.ShapeDtypeStruct(x.shape, _F32),
        scratch_shapes=[
            pltpu.VMEM((HGRN_HEADS, HGRN_KDIM, HGRN_KDIM), _F32),
            pltpu.VMEM((POOL_HALO + tt, d_b), _F32),
            pltpu.VMEM((POOL_HALO + tt, d_b), _F32),
            pltpu.VMEM((POOL_HALO + tt, d_b), _F32),
            pltpu.VMEM((tt, d), _BF16),
        ],
        compiler_params=pltpu.CompilerParams(
            dimension_semantics=("arbitrary", "arbitrary"), vmem_limit_bytes=VMEM_LIMIT_BYTES),
        name=f"mixer_l{l}",
    )(x, p['ffn1_norm'], p['ffn1_w1'], p['ffn1_w3'], p['ffn1_w2'], p['mix_norm'], p['w_in'], p['lb'],
      p['hgrn_norm'], p['pool_w'], p['pool_scale'], p['w_out'], lvl, tril)


def _attn_call(x, kv, l, p, tt, final_norm):
    bsz, t, d = x.shape
    f = p['ffn2_w1'].shape[2]
    n_mem = kv.shape[2]
    lsel = lambda *shape_tail: (lambda b, i: (l,) + (0,) * len(shape_tail))
    in_specs = [
        pl.BlockSpec((None, tt, d), lambda b, i: (b, i, 0)),
        pl.BlockSpec((None, None, n_mem, 2 * d), lambda b, i: (l, b, 0, 0)),
        _resident((None, 1, d), lsel(1, d)),
        _resident((None, d, d), lsel(d, d)),
        _resident((None, d, d), lsel(d, d)),
        _resident((None, 1, d), lsel(1, d)),
        _resident((None, d, f), lsel(d, f)),
        _resident((None, d, f), lsel(d, f)),
        _resident((None, f, d), lsel(f, d)),
        _resident((1, d), lambda b, i: (0, 0)),
    ]
    return pl.pallas_call(
        functools.partial(_attn_kernel, final_norm=final_norm),
        grid=(bsz, t // tt),
        in_specs=in_specs,
        out_specs=pl.BlockSpec((None, tt, d), lambda b, i: (b, i, 0)),
        out_shape=jax.ShapeDtypeStruct(x.shape, _F32),
        scratch_shapes=[pltpu.VMEM((tt, d), _BF16)],
        compiler_params=pltpu.CompilerParams(
            dimension_semantics=("arbitrary", "arbitrary"), vmem_limit_bytes=ATTN_VMEM_LIMIT_BYTES),
        name=f"attn_l{l}",
    )(x, kv, p['xa_norm'], p['xa_wq'], p['xa_wo'], p['ffn2_norm'], p['ffn2_w1'], p['ffn2_w3'], p['ffn2_w2'],
      p['final_norm'])


def _kv_call(mem, mem_norm, wkv):
    bsz, n_mem, d = mem.shape
    depth = wkv.shape[0]
    return pl.pallas_call(
        _kv_kernel,
        grid=(depth, bsz),
        in_specs=[
            pl.BlockSpec((None, n_mem, d), lambda l, b: (b, 0, 0)),
            pl.BlockSpec((None, 1, d), lambda l, b: (l, 0, 0)),
            pl.BlockSpec((None, d, 2 * d), lambda l, b: (l, 0, 0)),
        ],
        out_specs=pl.BlockSpec((None, None, n_mem, 2 * d), lambda l, b: (l, b, 0, 0)),
        out_shape=jax.ShapeDtypeStruct((depth, bsz, n_mem, 2 * d), _BF16),
        compiler_params=pltpu.CompilerParams(
            dimension_semantics=("arbitrary", "arbitrary"), vmem_limit_bytes=VMEM_LIMIT_BYTES),
        name="mem_kv",
    )(mem, mem_norm, wkv)


def _pool_block_diag(pool_w):
    depth, n_grp, ch, _ = pool_w.shape
    w = pool_w.astype(_BF16).reshape(depth, n_grp // HEADS_PER_DOT, HEADS_PER_DOT, ch, ch)
    eye = jnp.eye(HEADS_PER_DOT, dtype=_BF16)
    out = w[:, :, :, :, None, :] * eye[None, None, :, None, :, None]
    return out.reshape(depth, n_grp // HEADS_PER_DOT, HEADS_PER_DOT * ch, HEADS_PER_DOT * ch)


def kernel(x, mem, ffn1_norm, ffn1_w1, ffn1_w3, ffn1_w2, mix_norm, w_in, lb_logits, hgrn_norm, pool_w, pool_scale,
           w_out, xa_norm, mem_norm, xa_wq, xa_wkv, xa_wo, ffn2_norm, ffn2_w1, ffn2_w3, ffn2_w2, final_norm):
    depth = w_in.shape[0]
    t = x.shape[1]
    tt = min(TIME_TILE, t)
    chunk = min(HGRN_CHUNK, tt)
    lvl_np, tril_np, n_levels = _level_tables(chunk)
    lvl = jnp.asarray(lvl_np)
    tril = jnp.asarray(tril_np, dtype=_BF16)

    lb_all = jnp.cumsum(jax.nn.softmax(lb_logits.astype(_F32), axis=0), axis=0)
    lb_all = lb_all - lb_all[0:1]

    row = lambda a: a.astype(_F32)[:, None, :]
    bf = lambda a: a.astype(_BF16)
    p = {
        'ffn1_norm': row(ffn1_norm), 'ffn1_w1': bf(ffn1_w1), 'ffn1_w3': bf(ffn1_w3), 'ffn1_w2': bf(ffn1_w2),
        'mix_norm': row(mix_norm), 'w_in': bf(w_in), 'lb': row(lb_all), 'hgrn_norm': row(hgrn_norm),
        'pool_w': _pool_block_diag(pool_w), 'pool_scale': row(pool_scale), 'w_out': bf(w_out),
        'xa_norm': row(xa_norm), 'xa_wq': bf(xa_wq), 'xa_wo': bf(xa_wo),
        'ffn2_norm': row(ffn2_norm), 'ffn2_w1': bf(ffn2_w1), 'ffn2_w3': bf(ffn2_w3), 'ffn2_w2': bf(ffn2_w2),
        'final_norm': final_norm.astype(_F32)[None, :],
    }
    kv = _kv_call(mem, row(mem_norm), bf(xa_wkv))
    for l in range(depth):
        x = _mixer_call(x, l, p, lvl, tril, n_levels, tt, chunk)
        x = _attn_call(x, kv, l, p, min(ATTN_TIME_TILE, t), final_norm=(l == depth - 1))
    return x
```

```python
import functools

import jax
import jax.numpy as jnp
import numpy as np
from jax import lax
from jax.experimental import pallas as pl
from jax.experimental.pallas import tpu as pltpu

EPS = 1e-6
HGRN_HEADS = 4
HGRN_KDIM = 128
POOL_WINDOWS = (2, 4, 8, 16)
POOL_CH = 128
POOL_HALO = 32
XA_HEADS = 4
SUBLANES = 8
MXU_DIM = 256

TIME_TILE = 512
ATTN_TIME_TILE = 1024
HGRN_CHUNK = 128
HEADS_PER_DOT = MXU_DIM // HGRN_KDIM
VMEM_LIMIT_BYTES = 56 * 1024 * 1024
ATTN_VMEM_LIMIT_BYTES = 60 * 1024 * 1024

_BF16 = jnp.bfloat16
_F32 = jnp.float32


def _dot(a, b):
    return jnp.dot(a, b, preferred_element_type=_F32)


def _dot_nt(a, b):
    return lax.dot_general(a, b, (((1,), (1,)), ((), ())), preferred_element_type=_F32)


def _dot_tn(a, b):
    return lax.dot_general(a, b, (((0,), (0,)), ((), ())), preferred_element_type=_F32)


def _block_diag(blocks):
    n = len(blocks)
    zero = jnp.zeros_like(blocks[0])
    return jnp.concatenate(
        [jnp.concatenate([blocks[i] if i == j else zero for j in range(n)], axis=1) for i in range(n)], axis=0)


def _rmsnorm(x, g):
    ms = jnp.mean(x * x, axis=-1, keepdims=True)
    return x * lax.rsqrt(ms + EPS) * g


def _swiglu_half_step(x, g, w1_ref, w3_ref, w2_ref):
    hn = _rmsnorm(x, g).astype(_BF16)
    h1 = _dot(hn, w1_ref[...])
    h3 = _dot(hn, w3_ref[...])
    a = (h1 * jax.nn.sigmoid(h1) * h3).astype(_BF16)
    return x + 0.5 * _dot(a, w2_ref[...])


def _level_tables(chunk):
    n_levels = int(np.log2(chunk))
    t = np.arange(chunk)[:, None]
    s = np.arange(chunk)[None, :]
    x = np.bitwise_xor(t, s)
    hb = np.floor(np.log2(np.maximum(x, 1))).astype(np.int32)
    lvl = np.where(t > s, hb, np.where(t == s, n_levels, -1)).astype(np.int32)
    tril = (t >= s).astype(np.float32)
    return np.tile(lvl, (1, HEADS_PER_DOT)), tril, n_levels


def _boundary_rows(b, level_bit):
    c, w = b.shape
    m = 1 << level_bit
    assert m >= 2
    if 2 * m >= 2 * SUBLANES:
        nb = c // (2 * m)
        b3 = b.reshape(nb, 2 * m, w)
        return jnp.broadcast_to(b3[:, m - 1:m, :], (nb, 2 * m, w)).reshape(c, w)
    b3 = b.reshape(c // SUBLANES, SUBLANES, w)
    sub = lax.broadcasted_iota(jnp.int32, b3.shape, 1)

    def row(i):
        return jnp.broadcast_to(b3[:, i:i + 1, :], b3.shape)

    out = row(3) if m == 4 else jnp.where(sub < 4, row(1), row(5))
    return out.reshape(c, w)


def _hgrn_chunk(q, z, v, lb_rows, lvl, tril_bf, st_ref, n_levels):
    c, w = q.shape
    log_lb, log1m_lb, one_m_lb = lb_rows
    u = jnp.exp(-jnp.abs(z))
    one_p_u = 1.0 + u
    log_sig = jnp.minimum(z, 0.0) - jnp.log(one_p_u)
    cterm = log1m_lb + log_sig
    mx = jnp.maximum(log_lb, cterm)
    g = mx + jnp.log(1.0 + jnp.exp(-jnp.abs(log_lb - cterm)))
    kk = one_m_lb * (jnp.where(z >= 0.0, u, 1.0) / one_p_u)

    g_hi = g.astype(_BF16)
    r1 = g - g_hi.astype(_F32)
    g_mid = r1.astype(_BF16)
    g_lo = (r1 - g_mid.astype(_F32)).astype(_BF16)
    b = _dot(tril_bf, g_hi) + _dot(tril_bf, g_mid) + _dot(tril_bf, g_lo)
    yield

    q_bf = q.astype(_BF16)
    k_bf = kk.astype(_BF16)
    v_bf = v.astype(_BF16)
    qs = [q_bf]
    ks = [k_bf]
    odd = (lax.broadcasted_iota(jnp.int32, (c, w), 0) & 1) != 0
    for bit in range(n_levels):
        if bit == 0:
            e = jnp.where(odd, jnp.exp(g), 1.0).astype(_BF16)
        else:
            e = jnp.exp(-jnp.abs(b - _boundary_rows(b, bit))).astype(_BF16)
        qs.append(q_bf * e)
        ks.append(k_bf * e)
    level_ids = [n_levels] + list(range(n_levels))

    b_last = b[c - 1:c, :]
    q_in = (q * jnp.exp(b)).astype(_BF16)
    k_out = (kk * jnp.exp(b_last - b)).astype(_BF16)
    s_decay = jnp.exp(b_last)

    gw = HEADS_PER_DOT * HGRN_KDIM
    groups = []
    for grp in range(HGRN_HEADS // HEADS_PER_DOT):
        cols = slice(grp * gw, (grp + 1) * gw)
        heads = range(grp * HEADS_PER_DOT, (grp + 1) * HEADS_PER_DOT)
        hsl = [slice(h * HGRN_KDIM, (h + 1) * HGRN_KDIM) for h in heads]
        a = jnp.zeros((c, HEADS_PER_DOT * c), _F32)
        for lid, ql, kl in zip(level_ids, qs, ks):
            a = jnp.where(lvl == lid, _dot_nt(ql[:, cols], _block_diag([kl[:, s] for s in hsl])), a)
        groups.append((cols, heads, hsl, a.astype(_BF16)))
    yield

    outs = []
    for cols, heads, hsl, a_bf in groups:
        st = [st_ref[h] for h in heads]
        outs.append(_dot(a_bf, _block_diag([v_bf[:, s] for s in hsl]))
                    + _dot_nt(q_in[:, cols], _block_diag([s.astype(_BF16) for s in st])))
        for h, s_old, s in zip(heads, st, hsl):
            st_ref[h] = s_old * s_decay[:, s] + _dot_tn(v_bf[:, s], k_out[:, s])
    yield outs


def _mixer_kernel(x_ref, g1_ref, w1_ref, w3_ref, w2_ref, gm_ref, win_ref, lb_ref, hn_ref, pw_ref, ps_ref,
                  wout_ref, lvl_ref, tril_ref, o_ref, st_ref, su_ref, sp_ref, sq_ref, mix_ref, *, chunk, n_levels):
    tt = x_ref.shape[0]
    d_a = HGRN_HEADS * HGRN_KDIM
    hal = POOL_HALO
    t_step = pl.program_id(1)

    @pl.when(t_step == 0)
    def _():
        st_ref[...] = jnp.zeros_like(st_ref)
        su_ref[0:hal, :] = jnp.zeros((hal, su_ref.shape[1]), _F32)

    x1 = _swiglu_half_step(x_ref[...], g1_ref[...], w1_ref, w3_ref, w2_ref)
    h = _rmsnorm(x1, gm_ref[...]).astype(_BF16)
    proj = _dot(h, win_ref[...])

    lb = lb_ref[...]
    lb_rows = (jnp.log(lb), jnp.log1p(-lb), 1.0 - lb)
    lvl = lvl_ref[...]
    tril_bf = tril_ref[...]
    hgrn_g = hn_ref[...]
    row_slices = [slice(ci * chunk, (ci + 1) * chunk) for ci in range(tt // chunk)]
    chunks = [_hgrn_chunk(proj[rows, 0:d_a], proj[rows, d_a:2 * d_a], proj[rows, 2 * d_a:3 * d_a],
                          lb_rows, lvl, tril_bf, st_ref, n_levels) for rows in row_slices]
    for ck in chunks:
        next(ck)
    for ck in chunks:
        next(ck)
    chunk_outs = [next(ck) for ck in chunks]
    for rows, outs in zip(row_slices, chunk_outs):
        for grp, og in enumerate(outs):
            for hi in range(HEADS_PER_DOT):
                hh = grp * HEADS_PER_DOT + hi
                sl = slice(hh * HGRN_KDIM, (hh + 1) * HGRN_KDIM)
                o = og[:, hi * HGRN_KDIM:(hi + 1) * HGRN_KDIM]
                o = o * lax.rsqrt(jnp.mean(o * o, axis=-1, keepdims=True) + EPS) * hgrn_g[:, sl]
                gate = proj[rows, 3 * d_a + hh * HGRN_KDIM:3 * d_a + (hh + 1) * HGRN_KDIM]
                mix_ref[rows, sl] = (o * (gate * jax.nn.sigmoid(gate))).astype(_BF16)

    u = proj[:, 4 * d_a:]
    n_rows = hal + tt
    su_ref[hal:n_rows, :] = u
    sp_ref[8:n_rows, :] = su_ref[8:n_rows, :] + su_ref[7:n_rows - 1, :]
    c1 = POOL_CH
    sq_ref[16:n_rows, c1:] = sp_ref[16:n_rows, c1:] + sp_ref[14:n_rows - 2, c1:]
    c2 = 2 * POOL_CH
    sp_ref[24:n_rows, c2:] = sq_ref[24:n_rows, c2:] + sq_ref[20:n_rows - 4, c2:]
    c3 = 3 * POOL_CH
    sums = [sp_ref[hal:n_rows, 0:c1], sq_ref[hal:n_rows, c1:c2], sp_ref[hal:n_rows, c2:c3],
            sp_ref[hal:n_rows, c3:] + sp_ref[hal - 8:n_rows - 8, c3:]]
    su_ref[0:hal, :] = u[tt - hal:tt, :]
    t_glob = t_step * tt + lax.broadcasted_iota(jnp.int32, (tt, 1), 0) + 1
    ps = []
    for j, win in enumerate(POOL_WINDOWS):
        cnt = jnp.minimum(t_glob, win).astype(_F32)
        ps.append((sums[j] / cnt - u[:, j * POOL_CH:(j + 1) * POOL_CH]).astype(_BF16))
    for grp in range(len(POOL_WINDOWS) // HEADS_PER_DOT):
        cols = slice(d_a + grp * MXU_DIM, d_a + (grp + 1) * MXU_DIM)
        p_grp = jnp.concatenate(ps[grp * HEADS_PER_DOT:(grp + 1) * HEADS_PER_DOT], axis=1)
        y = _dot(p_grp, pw_ref[grp]) * ps_ref[:, grp * MXU_DIM:(grp + 1) * MXU_DIM]
        mix_ref[:, cols] = y.astype(_BF16)

    o_ref[...] = x1 + _dot(mix_ref[...], wout_ref[...])


def _attn_kernel(x_ref, kv_ref, gx_ref, wq_ref, wo_ref, g2_ref, w1_ref, w3_ref, w2_ref, gf_ref, o_ref, att_ref,
                 *, final_norm):
    x2 = x_ref[...]
    d = x2.shape[1]
    hd = d // XA_HEADS
    h = _rmsnorm(x2, gx_ref[...]).astype(_BF16)
    q = (_dot(h, wq_ref[...]) * (hd ** -0.5)).astype(_BF16)
    scores = [_dot_nt(q[:, hh * hd:(hh + 1) * hd], kv_ref[:, hh * hd:(hh + 1) * hd]) for hh in range(XA_HEADS)]
    probs = []
    for s in scores:
        p = jnp.exp(s - jnp.max(s, axis=-1, keepdims=True))
        probs.append((p / jnp.sum(p, axis=-1, keepdims=True)).astype(_BF16))
    for hh, p in enumerate(probs):
        v = kv_ref[:, d + hh * hd:d + (hh + 1) * hd]
        att_ref[:, hh * hd:(hh + 1) * hd] = _dot(p, v).astype(_BF16)
    x3 = x2 + _dot(att_ref[...], wo_ref[...])
    x4 = _swiglu_half_step(x3, g2_ref[...], w1_ref, w3_ref, w2_ref)
    if final_norm:
        x4 = _rmsnorm(x4, gf_ref[...])
    o_ref[...] = x4


def _kv_kernel(mem_ref, g_ref, wkv_ref, o_ref):
    mn = _rmsnorm(mem_ref[...], g_ref[...]).astype(_BF16)
    o_ref[...] = _dot(mn, wkv_ref[...]).astype(_BF16)


def _resident(shape, index_map):
    return pl.BlockSpec(shape, index_map, pipeline_mode=pl.Buffered(1))


def _mixer_call(x, l, p, lvl, tril, n_levels, tt, chunk):
    bsz, t, d = x.shape
    f = p['ffn1_w1'].shape[2]
    d_in = p['w_in'].shape[2]
    d_a = HGRN_HEADS * HGRN_KDIM
    d_b = d_in - 4 * d_a
    n_pool_grp = len(POOL_WINDOWS) // HEADS_PER_DOT
    lsel = lambda *shape_tail: (lambda b, i: (l,) + (0,) * len(shape_tail))
    in_specs = [
        pl.BlockSpec((None, tt, d), lambda b, i: (b, i, 0)),
        _resident((None, 1, d), lsel(1, d)),
        _resident((None, d, f), lsel(d, f)),
        _resident((None, d, f), lsel(d, f)),
        _resident((None, f, d), lsel(f, d)),
        _resident((None, 1, d), lsel(1, d)),
        _resident((None, d, d_in), lsel(d, d_in)),
        _resident((None, 1, d_a), lsel(1, d_a)),
        _resident((None, 1, d_a), lsel(1, d_a)),
        _resident((None, n_pool_grp, MXU_DIM, MXU_DIM), lsel(n_pool_grp, MXU_DIM, MXU_DIM)),
        _resident((None, 1, d_b), lsel(1, d_b)),
        _resident((None, d, d), lsel(d, d)),
        _resident(lvl.shape, lambda b, i: (0, 0)),
        _resident(tril.shape, lambda b, i: (0, 0)),
    ]
    return pl.pallas_call(
        functools.partial(_mixer_kernel, chunk=chunk, n_levels=n_levels),
        grid=(bsz, t // tt),
        in_specs=in_specs,
        out_specs=pl.BlockSpec((None, tt, d), lambda b, i: (b, i, 0)),
        out_shape=jax.ShapeDtypeStruct(x.shape, _F32),
        scratch_shapes=[
            pltpu.VMEM((HGRN_HEADS, HGRN_KDIM, HGRN_KDIM), _F32),
            pltpu.VMEM((POOL_HALO + tt, d_b), _F32),
            pltpu.VMEM((POOL_HALO + tt, d_b), _F32),
            pltpu.VMEM((POOL_HALO + tt, d_b), _F32),
            pltpu.VMEM((tt, d), _BF16),
        ],
        compiler_params=pltpu.CompilerParams(
            dimension_semantics=("arbitrary", "arbitrary"), vmem_limit_bytes=VMEM_LIMIT_BYTES),
        name=f"mixer_l{l}",
    )(x, p['ffn1_norm'], p['ffn1_w1'], p['ffn1_w3'], p['ffn1_w2'], p['mix_norm'], p['w_in'], p['lb'],
      p['hgrn_norm'], p['pool_w'], p['pool_scale'], p['w_out'], lvl, tril)


def _attn_call(x, kv, l, p, tt, final_norm):
    bsz, t, d = x.shape
    f = p['ffn2_w1'].shape[2]
    n_mem = kv.shape[2]
    lsel = lambda *shape_tail: (lambda b, i: (l,) + (0,) * len(shape_tail))
    in_specs = [
        pl.BlockSpec((None, tt, d), lambda b, i: (b, i, 0)),
        pl.BlockSpec((None, None, n_mem, 2 * d), lambda b, i: (l, b, 0, 0)),
        _resident((None, 1, d), lsel(1, d)),
        _resident((None, d, d), lsel(d, d)),
        _resident((None, d, d), lsel(d, d)),
        _resident((None, 1, d), lsel(1, d)),
        _resident((None, d, f), lsel(d, f)),
        _resident((None, d, f), lsel(d, f)),
        _resident((None, f, d), lsel(f, d)),
        _resident((1, d), lambda b, i: (0, 0)),
    ]
    return pl.pallas_call(
        functools.partial(_attn_kernel, final_norm=final_norm),
        grid=(bsz, t // tt),
        in_specs=in_specs,
        out_specs=pl.BlockSpec((None, tt, d), lambda b, i: (b, i, 0)),
        out_shape=jax.ShapeDtypeStruct(x.shape, _F32),
        scratch_shapes=[pltpu.VMEM((tt, d), _BF16)],
        compiler_params=pltpu.CompilerParams(
            dimension_semantics=("arbitrary", "arbitrary"), vmem_limit_bytes=ATTN_VMEM_LIMIT_BYTES),
        name=f"attn_l{l}",
    )(x, kv, p['xa_norm'], p['xa_wq'], p['xa_wo'], p['ffn2_norm'], p['ffn2_w1'], p['ffn2_w3'], p['ffn2_w2'],
      p['final_norm'])


def _kv_call(mem, mem_norm, wkv):
    bsz, n_mem, d = mem.shape
    depth = wkv.shape[0]
    return pl.pallas_call(
        _kv_kernel,
        grid=(depth, bsz),
        in_specs=[
            pl.BlockSpec((None, n_mem, d), lambda l, b: (b, 0, 0)),
            pl.BlockSpec((None, 1, d), lambda l, b: (l, 0, 0)),
            pl.BlockSpec((None, d, 2 * d), lambda l, b: (l, 0, 0)),
        ],
        out_specs=pl.BlockSpec((None, None, n_mem, 2 * d), lambda l, b: (l, b, 0, 0)),
        out_shape=jax.ShapeDtypeStruct((depth, bsz, n_mem, 2 * d), _BF16),
        compiler_params=pltpu.CompilerParams(
            dimension_semantics=("arbitrary", "arbitrary"), vmem_limit_bytes=VMEM_LIMIT_BYTES),
        name="mem_kv",
    )(mem, mem_norm, wkv)


def _pool_block_diag(pool_w):
    depth, n_grp, ch, _ = pool_w.shape
    w = pool_w.astype(_BF16).reshape(depth, n_grp // HEADS_PER_DOT, HEADS_PER_DOT, ch, ch)
    eye = jnp.eye(HEADS_PER_DOT, dtype=_BF16)
    out = w[:, :, :, :, None, :] * eye[None, None, :, None, :, None]
    return out.reshape(depth, n_grp // HEADS_PER_DOT, HEADS_PER_DOT * ch, HEADS_PER_DOT * ch)


def kernel(x, mem, ffn1_norm, ffn1_w1, ffn1_w3, ffn1_w2, mix_norm, w_in, lb_logits, hgrn_norm, pool_w, pool_scale,
           w_out, xa_norm, mem_norm, xa_wq, xa_wkv, xa_wo, ffn2_norm, ffn2_w1, ffn2_w3, ffn2_w2, final_norm):
    depth = w_in.shape[0]
    t = x.shape[1]
    tt = min(TIME_TILE, t)
    chunk = min(HGRN_CHUNK, tt)
    lvl_np, tril_np, n_levels = _level_tables(chunk)
    lvl = jnp.asarray(lvl_np)
    tril = jnp.asarray(tril_np, dtype=_BF16)

    lb_all = jnp.cumsum(jax.nn.softmax(lb_logits.astype(_F32), axis=0), axis=0)
    lb_all = lb_all - lb_all[0:1]

    row = lambda a: a.astype(_F32)[:, None, :]
    bf = lambda a: a.astype(_BF16)
    p = {
        'ffn1_norm': row(ffn1_norm), 'ffn1_w1': bf(ffn1_w1), 'ffn1_w3': bf(ffn1_w3), 'ffn1_w2': bf(ffn1_w2),
        'mix_norm': row(mix_norm), 'w_in': bf(w_in), 'lb': row(lb_all), 'hgrn_norm': row(hgrn_norm),
        'pool_w': _pool_block_diag(pool_w), 'pool_scale': row(pool_scale), 'w_out': bf(w_out),
        'xa_norm': row(xa_norm), 'xa_wq': bf(xa_wq), 'xa_wo': bf(xa_wo),
        'ffn2_norm': row(ffn2_norm), 'ffn2_w1': bf(ffn2_w1), 'ffn2_w3': bf(ffn2_w3), 'ffn2_w2': bf(ffn2_w2),
        'final_norm': final_norm.astype(_F32)[None, :],
    }
    kv = _kv_call(mem, row(mem_norm), bf(xa_wkv))
    for l in range(depth):
        x = _mixer_call(x, l, p, lvl, tril, n_levels, tt, chunk)
        x = _attn_call(x, kv, l, p, min(ATTN_TIME_TILE, t), final_norm=(l == depth - 1))
    return x
```
